```python
import jax
import jax.numpy as jnp
from jax import lax
import numpy as np

D_MODEL = 2048
BATCH = 8
SEQ = 2048
DEPTH = 2

HEAD_DIM = 64
GRID_W = 64
Q_BLOCK = 128
ROPE_THETA = 10000.0
NEG_INF = -1e30
EPS = 1e-6
N_BRANCH = 4
BRANCH_WIDTH = 512

A_HEADS = 8
A_DILATIONS = ((128, 1), (512, 4), (2048, 16))
N_DIL = 3
MLA_HEADS = 8
MLA_Q_LORA = 512
MLA_KV_LORA = 256
MLA_NOPE = 64
MLA_ROPE = 32
MLA_V = 64
C_Q_HEADS = 8
C_KV_HEADS = 2
D_Q_HEADS = 8
D_KV_HEADS = 2
D_HALF_WINDOW = 128

A_COLS = 3 * N_DIL * A_HEADS * HEAD_DIM
B_COLS = MLA_Q_LORA + MLA_KV_LORA + MLA_ROPE
C_COLS = (C_Q_HEADS + 2 * C_KV_HEADS) * HEAD_DIM
D_COLS = (D_Q_HEADS + 2 * D_KV_HEADS) * HEAD_DIM
GATE_COLS = N_BRANCH * D_MODEL
IN_SPLITS = (A_COLS, A_COLS + B_COLS, A_COLS + B_COLS + C_COLS, A_COLS + B_COLS + C_COLS + D_COLS)
N_IN = A_COLS + B_COLS + C_COLS + D_COLS + GATE_COLS

N_EXPERTS = 16
EXPERT_FF = D_MODEL // 2
CAPACITY_FACTOR = 2

kernel_name = 'hybrid_gated_mixer_ec_moe_encoder'


def rms_norm(x, g):
    xf = x.astype(jnp.float32)
    y = xf * lax.rsqrt(jnp.mean(xf * xf, axis=-1, keepdims=True) + EPS)
    return (y * g.astype(jnp.float32)).astype(x.dtype)


def alibi_slopes(n):
    return jnp.exp2(-8.0 * jnp.arange(1, n + 1, dtype=jnp.float32) / n)


def rope(x, pos):
    dim = x.shape[-1]
    freqs = ROPE_THETA ** (-jnp.arange(0, dim, 2, dtype=jnp.float32) / dim)
    ang = pos.astype(jnp.float32)[:, None] * freqs[None, :]
    cos = jnp.cos(ang)[None, :, None, :]
    sin = jnp.sin(ang)[None, :, None, :]
    xf = x.astype(jnp.float32)
    x1, x2 = xf[..., : dim // 2], xf[..., dim // 2:]
    return jnp.concatenate([x1 * cos - x2 * sin, x1 * sin + x2 * cos], axis=-1).astype(x.dtype)


def axial_rope(x, rows, cols):
    half = x.shape[-1] // 2
    return jnp.concatenate([rope(x[..., :half], rows), rope(x[..., half:], cols)], axis=-1)


def banded_attention(q, k, v, half_window, slopes, sink=None):
    Z, L, H, G, dh = q.shape
    W = half_window
    nb = -(-L // W)
    Lp = nb * W
    pad = Lp - L
    qb = jnp.pad(q, ((0, 0), (0, pad), (0, 0), (0, 0), (0, 0))).reshape(Z, nb, W, H, G, dh)
    kp = jnp.pad(k, ((0, 0), (W, pad + W), (0, 0), (0, 0)))
    vp = jnp.pad(v.astype(jnp.float32), ((0, 0), (W, pad + W), (0, 0), (0, 0)))

    def key_blocks(a):
        return jnp.concatenate([a[:, i * W: i * W + Lp].reshape(Z, nb, W, H, a.shape[-1]) for i in range(3)], axis=2)

    kb, vb = key_blocks(kp), key_blocks(vp)
    qpos = jnp.arange(nb)[:, None] * W + jnp.arange(W)[None, :]
    kpos = jnp.arange(nb)[:, None] * W - W + jnp.arange(3 * W)[None, :]
    rel = kpos[:, None, :] - qpos[:, :, None]
    valid = (jnp.abs(rel) <= W) & (kpos[:, None, :] >= 0) & (kpos[:, None, :] < L)
    dist = jnp.abs(rel).astype(jnp.float32)
    s = jnp.einsum('znqhgd,znkhd->znhgqk', qb, kb).astype(jnp.float32) * (dh ** -0.5)
    s = s - slopes[None, None, :, :, None, None] * dist[None, :, None, None, :, :]
    s = jnp.where(valid[None, :, None, None, :, :], s, NEG_INF)
    m = jnp.max(s, axis=-1)
    if sink is not None:
        sk = sink.astype(jnp.float32)[None, None, :, :, None]
        m = jnp.maximum(m, sk)
    p = jnp.exp(s - m[..., None])
    den = jnp.sum(p, axis=-1)
    if sink is not None:
        den = den + jnp.exp(sk - m)
    o = jnp.einsum('znhgqk,znkhe->znqhge', p, vb) / jnp.transpose(den, (0, 1, 4, 2, 3))[..., None]
    lse = jnp.transpose(m + jnp.log(den), (0, 1, 4, 2, 3))
    o = o.reshape(Z, Lp, H, G, v.shape[-1])[:, :L]
    lse = lse.reshape(Z, Lp, H, G)[:, :L]
    return o.astype(q.dtype), lse


def dense_attention(q, k, v, scale):
    B, S, H, G, dq = q.shape
    nq = S // Q_BLOCK
    qb = jnp.moveaxis(q.reshape(B, nq, Q_BLOCK, H, G, dq), 1, 0)
    vf = v.astype(jnp.float32)

    def block(qi):
        s = jnp.einsum('bqhgd,bkhd->bhgqk', qi, k).astype(jnp.float32) * scale
        p = jax.nn.softmax(s, axis=-1)
        return jnp.einsum('bhgqk,bkhe->bqhge', p, vf).astype(q.dtype)

    out = lax.map(block, qb)
    return jnp.moveaxis(out, 0, 1).reshape(B, S, H, G, v.shape[-1])


def dilated_attention(q, k, v):
    B, S = q.shape[:2]
    slopes = alibi_slopes(A_HEADS)
    outs, lses = [], []
    for g, (window, dil) in enumerate(A_DILATIONS):
        half = (window // 2) // dil
        Ld = S // dil

        def fold(a):
            return jnp.transpose(a[:, :, g].reshape(B, Ld, dil, A_HEADS, HEAD_DIM), (0, 2, 1, 3, 4)).reshape(B * dil, Ld, A_HEADS, HEAD_DIM)

        o, lse = banded_attention(fold(q)[:, :, :, None, :], fold(k), fold(v), half, (slopes * dil)[:, None])
        o = jnp.transpose(o[:, :, :, 0].reshape(B, dil, Ld, A_HEADS, HEAD_DIM), (0, 2, 1, 3, 4)).reshape(B, S, A_HEADS, HEAD_DIM)
        lse = jnp.transpose(lse[:, :, :, 0].reshape(B, dil, Ld, A_HEADS), (0, 2, 1, 3)).reshape(B, S, A_HEADS)
        outs.append(o)
        lses.append(lse)
    w = jax.nn.softmax(jnp.stack(lses), axis=0)
    o = jnp.einsum('nbsh,nbshd->bshd', w, jnp.stack(outs).astype(jnp.float32))
    return o.reshape(B, S, A_HEADS * HEAD_DIM).astype(q.dtype)


def token_mixer(xn, w_in, b_gate, g_mla_q, g_mla_kv, w_mla_uq, w_mla_ukv, g_c_q, g_c_k, sink_d, w_branch, w_out):
    B, S, D = xn.shape
    rows_count = S // GRID_W
    pos = jnp.arange(S)
    rows = jnp.repeat(jnp.arange(rows_count), GRID_W)
    cols = jnp.tile(jnp.arange(GRID_W), rows_count)

    proj = xn @ w_in
    pa, pb, pc, pd, pg = jnp.split(proj, IN_SPLITS, axis=-1)

    pa = pa.reshape(B, S, 3, N_DIL, A_HEADS, HEAD_DIM)
    o_a = dilated_attention(pa[:, :, 0], pa[:, :, 1], pa[:, :, 2])

    c_q, c_kv, k_pe = jnp.split(pb, (MLA_Q_LORA, MLA_Q_LORA + MLA_KV_LORA), axis=-1)
    q_b = (rms_norm(c_q, g_mla_q) @ w_mla_uq).reshape(B, S, MLA_HEADS, MLA_NOPE + MLA_ROPE)
    q_b = jnp.concatenate([q_b[..., :MLA_NOPE], rope(q_b[..., MLA_NOPE:], pos)], axis=-1)
    kv_b = (rms_norm(c_kv, g_mla_kv) @ w_mla_ukv).reshape(B, S, MLA_HEADS, MLA_NOPE + MLA_V)
    k_pe = jnp.broadcast_to(rope(k_pe[:, :, None, :], pos), (B, S, MLA_HEADS, MLA_ROPE))
    k_b = jnp.concatenate([kv_b[..., :MLA_NOPE], k_pe], axis=-1)
    v_b = kv_b[..., MLA_NOPE:]
    o_b = dense_attention(q_b[:, :, :, None, :], k_b, v_b, (MLA_NOPE + MLA_ROPE) ** -0.5).reshape(B, S, MLA_HEADS * MLA_V)

    pc = pc.reshape(B, S, C_Q_HEADS + 2 * C_KV_HEADS, HEAD_DIM)
    q_c = axial_rope(rms_norm(pc[:, :, :C_Q_HEADS], g_c_q), rows, cols)
    k_c = axial_rope(rms_norm(pc[:, :, C_Q_HEADS:C_Q_HEADS + C_KV_HEADS], g_c_k), rows, cols)
    v_c = pc[:, :, C_Q_HEADS + C_KV_HEADS:]
    q_c = q_c.reshape(B, S, C_KV_HEADS, C_Q_HEADS // C_KV_HEADS, HEAD_DIM)
    o_c = dense_attention(q_c, k_c, v_c, HEAD_DIM ** -0.5).reshape(B, S, C_Q_HEADS * HEAD_DIM)

    pd = pd.reshape(B, S, D_Q_HEADS + 2 * D_KV_HEADS, HEAD_DIM)
    grp = D_Q_HEADS // D_KV_HEADS
    q_d = pd[:, :, :D_Q_HEADS].reshape(B, S, D_KV_HEADS, grp, HEAD_DIM)
    k_d = pd[:, :, D_Q_HEADS:D_Q_HEADS + D_KV_HEADS]
    v_d = pd[:, :, D_Q_HEADS + D_KV_HEADS:]
    o_d, _ = banded_attention(q_d, k_d, v_d, D_HALF_WINDOW, alibi_slopes(D_Q_HEADS).reshape(D_KV_HEADS, grp), sink_d.reshape(D_KV_HEADS, grp))
    o_d = o_d.reshape(B, S, D_Q_HEADS * HEAD_DIM)

    mixed = jnp.zeros((B, S, D), xn.dtype)
    for i, o in enumerate((o_a, o_b, o_c, o_d)):
        gate = jax.nn.sigmoid(pg[..., i * D:(i + 1) * D] + b_gate[i])
        mixed = mixed + gate * (o @ w_branch[i])
    return mixed @ w_out


def expert_choice_ffn(xn, w_router, w_gate, w_up, w_down):
    B, S, D = xn.shape
    cap = CAPACITY_FACTOR * S // N_EXPERTS
    probs = jax.nn.softmax((xn @ w_router).astype(jnp.float32), axis=-1)
    gate, idx = lax.top_k(jnp.transpose(probs, (0, 2, 1)), cap)
    xe = jax.vmap(lambda xb, ib: xb[ib])(xn, idx)
    h = jax.nn.silu(jnp.einsum('becd,edf->becf', xe, w_gate)) * jnp.einsum('becd,edf->becf', xe, w_up)
    ye = jnp.einsum('becf,efd->becd', h, w_down) * gate[..., None].astype(xn.dtype)
    return jax.vmap(lambda yb, ib: jax.ops.segment_sum(yb.reshape(-1, D), ib.reshape(-1), num_segments=S))(ye, idx)


def setup_inputs(seed: int = 0) -> dict:
    key = jax.random.key(seed)
    ks = jax.random.split(key, 20)
    f32 = jnp.float32

    def nrm(k, shape, scale):
        return jax.random.normal(k, shape, f32) * scale

    return {
        'x': nrm(ks[0], (BATCH, SEQ, D_MODEL), 1.0),
        'w_in': nrm(ks[1], (DEPTH, D_MODEL, N_IN), D_MODEL ** -0.5),
        'b_gate': nrm(ks[2], (DEPTH, N_BRANCH, D_MODEL), 0.02),
        'g_attn_norm': 1.0 + nrm(ks[3], (DEPTH, D_MODEL), 0.02),
        'g_ffn_norm': 1.0 + nrm(ks[4], (DEPTH, D_MODEL), 0.02),
        'g_mla_q': 1.0 + nrm(ks[5], (DEPTH, MLA_Q_LORA), 0.02),
        'g_mla_kv': 1.0 + nrm(ks[6], (DEPTH, MLA_KV_LORA), 0.02),
        'w_mla_uq': nrm(ks[7], (DEPTH, MLA_Q_LORA, MLA_HEADS * (MLA_NOPE + MLA_ROPE)), MLA_Q_LORA ** -0.5),
        'w_mla_ukv': nrm(ks[8], (DEPTH, MLA_KV_LORA, MLA_HEADS * (MLA_NOPE + MLA_V)), MLA_KV_LORA ** -0.5),
        'g_c_q': 1.0 + nrm(ks[9], (DEPTH, HEAD_DIM), 0.02),
        'g_c_k': 1.0 + nrm(ks[10], (DEPTH, HEAD_DIM), 0.02),
        'sink_d': nrm(ks[11], (DEPTH, D_Q_HEADS), 0.5),
        'w_branch': nrm(ks[12], (DEPTH, N_BRANCH, BRANCH_WIDTH, D_MODEL), BRANCH_WIDTH ** -0.5),
        'w_out': nrm(ks[13], (DEPTH, D_MODEL, D_MODEL), D_MODEL ** -0.5),
        'w_router': nrm(ks[14], (DEPTH, D_MODEL, N_EXPERTS), D_MODEL ** -0.5),
        'w_exp_gate': nrm(ks[15], (DEPTH, N_EXPERTS, D_MODEL, EXPERT_FF), D_MODEL ** -0.5),
        'w_exp_up': nrm(ks[16], (DEPTH, N_EXPERTS, D_MODEL, EXPERT_FF), D_MODEL ** -0.5),
        'w_exp_down': nrm(ks[17], (DEPTH, N_EXPERTS, EXPERT_FF, D_MODEL), EXPERT_FF ** -0.5),
        'g_final': 1.0 + nrm(ks[18], (D_MODEL,), 0.02),
    }


def reference(x, w_in, b_gate, g_attn_norm, g_ffn_norm, g_mla_q, g_mla_kv, w_mla_uq, w_mla_ukv, g_c_q, g_c_k, sink_d, w_branch, w_out, w_router, w_exp_gate, w_exp_up, w_exp_down, g_final):
    h = x
    for l in range(DEPTH):
        h = h + token_mixer(rms_norm(h, g_attn_norm[l]), w_in[l], b_gate[l], g_mla_q[l], g_mla_kv[l], w_mla_uq[l], w_mla_ukv[l], g_c_q[l], g_c_k[l], sink_d[l], w_branch[l], w_out[l])
        h = h + expert_choice_ffn(rms_norm(h, g_ffn_norm[l]), w_router[l], w_exp_gate[l], w_exp_up[l], w_exp_down[l])
    return rms_norm(h, g_final)
```

```python
import functools

import jax
import jax.numpy as jnp
from jax import lax
from jax.experimental import pallas as pl
from jax.experimental.pallas import tpu as pltpu

D_MODEL = 2048
SEQ = 2048
DEPTH = 2
HEAD_DIM = 64
GRID_W = 64
ROPE_THETA = 10000.0
NEG_INF = -1e30
EPS = 1e-6
N_BRANCH = 4
BRANCH_WIDTH = 512

A_HEADS = 8
A_DILATIONS = ((128, 1), (512, 4), (2048, 16))
N_DIL = 3
MLA_HEADS = 8
MLA_Q_LORA = 512
MLA_KV_LORA = 256
MLA_NOPE = 64
MLA_ROPE = 32
MLA_V = 64
C_Q_HEADS = 8
C_KV_HEADS = 2
D_Q_HEADS = 8
D_KV_HEADS = 2
D_HALF_WINDOW = 128

A_COLS = 3 * N_DIL * A_HEADS * HEAD_DIM
B_COLS = MLA_Q_LORA + MLA_KV_LORA + MLA_ROPE
C_COLS = (C_Q_HEADS + 2 * C_KV_HEADS) * HEAD_DIM
D_COLS = (D_Q_HEADS + 2 * D_KV_HEADS) * HEAD_DIM
GATE_COL0 = A_COLS + B_COLS + C_COLS + D_COLS

N_EXPERTS = 16
EXPERT_FF = D_MODEL // 2
CAPACITY = 2 * SEQ // N_EXPERTS

LANES = 128
N_PAIRS = 4
VMEM_LIMIT = 56 * 1024 * 1024

GROUP_COLS = 3 * A_HEADS * HEAD_DIM
REST_COLS = 3072
GATE_COLS = N_BRANCH * D_MODEL
BLK_CQ = 0
BLK_QC = 4
BLK_QD = 8
BLK_CKV = 12
BLK_KPE = 14
BLK_KC = 16
BLK_VC = 18
BLK_KD = 20
BLK_VD = 22


def _cparams(n_grid_dims):
    return pltpu.CompilerParams(dimension_semantics=("arbitrary",) * n_grid_dims, vmem_limit_bytes=VMEM_LIMIT)


def _proj_segments():
    segs = []
    for g in range(N_DIL):
        for s in range(3):
            segs.append(((s * N_DIL + g) * A_HEADS * HEAD_DIM, A_HEADS * HEAD_DIM))
    b0 = A_COLS
    c0 = A_COLS + B_COLS
    d0 = c0 + C_COLS
    segs.append((b0, MLA_Q_LORA))
    segs.append((c0, C_Q_HEADS * HEAD_DIM))
    segs.append((d0, D_Q_HEADS * HEAD_DIM))
    segs.append((b0 + MLA_Q_LORA, MLA_KV_LORA))
    segs += [(None, 64), (b0 + MLA_Q_LORA + MLA_KV_LORA, MLA_ROPE), (None, 32 + LANES)]
    for base, nq, nkv in ((c0, C_Q_HEADS, C_KV_HEADS), (d0, D_Q_HEADS, D_KV_HEADS)):
        for part in range(2):
            for kv in range(nkv):
                start = base + (nq + part * nkv + kv) * HEAD_DIM
                segs += [(start, HEAD_DIM), (start, HEAD_DIM)]
    assert sum(w for _, w in segs) == N_DIL * GROUP_COLS + REST_COLS
    segs.append((GATE_COL0, GATE_COLS))
    return segs


def _relayout_w_in(w_in):
    parts = []
    for start, width in _proj_segments():
        if start is None:
            parts.append(jnp.zeros(w_in.shape[:2] + (width,), jnp.bfloat16))
        else:
            parts.append(w_in[:, :, start:start + width].astype(jnp.bfloat16))
    return jnp.concatenate(parts, axis=2)


def _mla_uq_layout(w_uq):
    w = w_uq.reshape(MLA_Q_LORA, MLA_HEADS, MLA_NOPE + MLA_ROPE)
    w = jnp.pad(w, ((0, 0), (0, 0), (0, LANES - MLA_NOPE - MLA_ROPE)))
    return w.reshape(MLA_Q_LORA, MLA_HEADS * LANES)


def _mla_ukv_layout(w_ukv):
    w = w_ukv.reshape(MLA_KV_LORA, MLA_HEADS, MLA_NOPE + MLA_V)
    wk = jnp.pad(w[:, :, :MLA_NOPE], ((0, 0), (0, 0), (0, LANES - MLA_NOPE)))
    wv = w[:, :, MLA_NOPE:]
    return jnp.concatenate([wk.reshape(MLA_KV_LORA, MLA_HEADS * LANES),
                            wv.reshape(MLA_KV_LORA, MLA_HEADS * MLA_V)], axis=1)


def _rope_tables(pos_list, dim, lane_offsets):
    half = dim // 2
    freqs = ROPE_THETA ** (-jnp.arange(0, dim, 2, dtype=jnp.float32) / dim)
    cos_t = jnp.ones((SEQ, LANES), jnp.float32)
    sin_a = jnp.zeros((SEQ, LANES), jnp.float32)
    sin_b = jnp.zeros((SEQ, LANES), jnp.float32)
    for pos, off in zip(pos_list, lane_offsets):
        ang = pos.astype(jnp.float32)[:, None] * freqs[None, :]
        c, s = jnp.cos(ang), jnp.sin(ang)
        cos_t = cos_t.at[:, off:off + half].set(c).at[:, off + half:off + dim].set(c)
        sin_a = sin_a.at[:, off:off + half].set(-s)
        sin_b = sin_b.at[:, off + half:off + dim].set(s)
    return cos_t, sin_a, sin_b


def _apply_rope(x, cos_t, sin_a, sin_b, half):
    n = x.shape[-1]
    return (x * cos_t + pltpu.roll(x, n - half, 1) * sin_a + pltpu.roll(x, half, 1) * sin_b)


def _rmsnorm_kernel(x_ref, g_ref, o_ref):
    x = x_ref[...]
    ms = jnp.mean(x * x, axis=-1, keepdims=True)
    o_ref[...] = (x * lax.rsqrt(ms + EPS) * g_ref[...]).astype(o_ref.dtype)


def _rmsnorm(x, g, out_dtype, tm=512):
    t, d = x.shape
    return pl.pallas_call(
        _rmsnorm_kernel,
        grid=(t // tm,),
        in_specs=[pl.BlockSpec((tm, d), lambda i: (i, 0)), pl.BlockSpec((1, d), lambda i: (0, 0))],
        out_specs=pl.BlockSpec((tm, d), lambda i: (i, 0)),
        out_shape=jax.ShapeDtypeStruct((t, d), out_dtype),
        compiler_params=_cparams(1),
        name="rmsnorm",
    )(x, g.reshape(1, d))


def _matmul_kernel(x_ref, w_ref, o_ref):
    o_ref[...] = jnp.dot(x_ref[...], w_ref[...], preferred_element_type=jnp.float32).astype(o_ref.dtype)


def _proj_rest(x, w_all, layer, tm, tn=1536):
    m, k = x.shape
    blk0 = N_DIL * GROUP_COLS // tn
    return pl.pallas_call(
        _matmul_kernel,
        grid=(m // tm, REST_COLS // tn),
        in_specs=[pl.BlockSpec((tm, k), lambda i, j: (i, 0)),
                  pl.BlockSpec((None, k, tn), lambda i, j: (layer, 0, blk0 + j))],
        out_specs=pl.BlockSpec((tm, tn), lambda i, j: (i, j)),
        out_shape=jax.ShapeDtypeStruct((m, REST_COLS), jnp.bfloat16),
        compiler_params=_cparams(2),
        name="in_proj",
    )(x, w_all)


def _proj_dilated_kernel(x_ref, w0_ref, w1_ref, w2_ref, o0_ref, o1_ref, o2_ref, scr_ref):
    x = x_ref[...]
    for w_ref, o_ref, (_, dil) in zip((w0_ref, w1_ref, w2_ref), (o0_ref, o1_ref, o2_ref), A_DILATIONS):
        acc = jnp.dot(x, w_ref[...], preferred_element_type=jnp.float32)
        if dil == 1:
            o_ref[0] = acc.astype(o_ref.dtype)
        else:
            n_blk = acc.shape[1] // LANES
            for c in range(n_blk):
                scr_ref[c] = acc[:, c * LANES:(c + 1) * LANES]
            for r in range(dil):
                for c in range(n_blk):
                    rows = scr_ref[c, pl.ds(r, SEQ // dil, stride=dil), :]
                    o_ref[r, :, c * LANES:(c + 1) * LANES] = rows.astype(o_ref.dtype)


def _proj_dilated(x, w_all, layer, tn=512):
    t, k = x.shape
    b = t // SEQ
    nc = GROUP_COLS // tn

    def w_spec(g):
        return pl.BlockSpec((None, k, tn), lambda i, c, g=g: (layer, 0, g * nc + c))

    out_specs, out_shape = [], []
    for _, dil in A_DILATIONS:
        ld = SEQ // dil
        out_specs.append(pl.BlockSpec((None, dil, ld, tn), lambda i, c: (i, 0, 0, c)))
        out_shape.append(jax.ShapeDtypeStruct((b, dil, ld, GROUP_COLS), jnp.bfloat16))
    return pl.pallas_call(
        _proj_dilated_kernel,
        grid=(b, nc),
        in_specs=[pl.BlockSpec((SEQ, k), lambda i, c: (i, 0)), w_spec(0), w_spec(1), w_spec(2)],
        out_specs=out_specs,
        out_shape=out_shape,
        scratch_shapes=[pltpu.VMEM((tn // LANES, SEQ, LANES), jnp.float32)],
        compiler_params=_cparams(2),
        name="in_proj_dilated",
    )(x, w_all, w_all, w_all)


def _banded_kernel(slope_ref, sink_ref, q_ref, k_ref, v_ref, *refs, length, half_window, dil, use_sink, want_lse):
    o_ref = refs[0]
    lse_ref = refs[1] if want_lse else None
    bias_ref = refs[-1]
    hp = pl.program_id(1)
    w = half_window
    tq = 2 * w
    tk = min(4 * w, length)
    nq = length // tq
    key_start = [min(max(qb * tq - w, 0), length - tk) for qb in range(nq)]
    offsets = sorted({key_start[qb] - qb * tq for qb in range(nq)})

    lane = lax.broadcasted_iota(jnp.int32, (1, LANES), 1)
    half0 = lane < HEAD_DIM
    delta = lax.broadcasted_iota(jnp.int32, (tq, tk), 1) - lax.broadcasted_iota(jnp.int32, (tq, tk), 0)
    slopes = [slope_ref[2 * hp + i] for i in range(2)]
    for oi, off in enumerate(offsets):
        rel = jnp.abs(delta + off)
        dist = rel.astype(jnp.float32)
        for i in range(2):
            bias_ref[2 * oi + i] = jnp.where(rel <= w, -slopes[i] * dist, NEG_INF)

    scale = jnp.asarray(HEAD_DIM ** -0.5, q_ref.dtype)
    for r in range(dil):
        for qb in range(nq):
            qs, ks = qb * tq, key_start[qb]
            oi = offsets.index(ks - qs)
            q = q_ref[r, qs:qs + tq, :] * scale
            kw = k_ref[r, ks:ks + tk, :]
            vw = v_ref[r, ks:ks + tk, :]
            outs, lses = [], []
            for i in range(2):
                qi = jnp.where(half0 if i == 0 else jnp.logical_not(half0), q, jnp.zeros_like(q))
                s = lax.dot_general(qi, kw, (((1,), (1,)), ((), ())), preferred_element_type=jnp.float32)
                s = s + bias_ref[2 * oi + i]
                m = jnp.max(s, axis=-1, keepdims=True)
                if use_sink:
                    sk = sink_ref[2 * hp + i]
                    m = jnp.maximum(m, sk)
                p = jnp.exp(s - m)
                den = jnp.sum(p, axis=-1, keepdims=True)
                if use_sink:
                    den = den + jnp.exp(sk - m)
                pv = jnp.dot(p.astype(vw.dtype), vw, preferred_element_type=jnp.float32)
                outs.append(pv / den)
                lses.append(m + jnp.log(den))
            rows = pl.ds(qs, tq) if dil == 1 else pl.ds(r + qs * dil, tq, stride=dil)
            o_ref[rows, :] = jnp.where(half0, outs[0], outs[1]).astype(o_ref.dtype)
            if want_lse:
                lse_ref[rows, :] = jnp.where(half0, lses[0], lses[1]).astype(lse_ref.dtype)


def _banded(qkv, blocks, slopes, sinks, half_window, use_sink, want_lse, out_dtype, name):
    b, dil, length, _ = qkv.shape
    tq = 2 * half_window
    tk = min(4 * half_window, length)
    n_off = 1 if tk == length else 3
    kern = functools.partial(_banded_kernel, length=length, half_window=half_window, dil=dil,
                             use_sink=use_sink, want_lse=want_lse)
    smem = pl.BlockSpec(memory_space=pltpu.SMEM)

    def spec(s):
        return pl.BlockSpec((None, dil, length, LANES), lambda i, j, s=s: (i, 0, 0, blocks(j)[s]))

    out_spec = pl.BlockSpec((SEQ, LANES), lambda i, j: (i, j))
    out_specs = [out_spec]
    out_shape = [jax.ShapeDtypeStruct((b * SEQ, N_PAIRS * LANES), out_dtype)]
    if want_lse:
        out_specs.append(out_spec)
        out_shape.append(jax.ShapeDtypeStruct((b * SEQ, N_PAIRS * LANES), jnp.float32))
    return pl.pallas_call(
        kern,
        grid=(b, N_PAIRS),
        in_specs=[smem, smem, spec(0), spec(1), spec(2)],
        out_specs=out_specs,
        out_shape=out_shape,
        scratch_shapes=[pltpu.VMEM((2 * n_off, tq, tk), jnp.float32)],
        compiler_params=_cparams(2),
        name=name,
    )(slopes, sinks, qkv, qkv, qkv)


def _a_combine_kernel(o0, o1, o2, l0, l1, l2, out_ref):
    la, lb, lc = l0[...], l1[...], l2[...]
    m = jnp.maximum(jnp.maximum(la, lb), lc)
    ea, eb, ec = jnp.exp(la - m), jnp.exp(lb - m), jnp.exp(lc - m)
    num = ea * o0[...] + eb * o1[...] + ec * o2[...]
    out_ref[...] = (num / (ea + eb + ec)).astype(out_ref.dtype)


def _a_combine(outs, lses, tm=1024):
    t, w = outs[0].shape
    spec = pl.BlockSpec((tm, w), lambda i: (i, 0))
    return pl.pallas_call(
        _a_combine_kernel,
        grid=(t // tm,),
        in_specs=[spec] * 6,
        out_specs=spec,
        out_shape=jax.ShapeDtypeStruct((t, w), jnp.bfloat16),
        compiler_params=_cparams(1),
        name="dilated_combine",
    )(*outs, *lses)


def _mla_prep_kernel(cq_ref, ckv_ref, kpe_ref, gq_ref, gkv_ref, wuq_ref, wukv_ref, cos_ref, sa_ref, sb_ref,
                     q_out, k_out, v_out, *, scale):
    def norm(x, g):
        x = x.astype(jnp.float32)
        return (x * lax.rsqrt(jnp.mean(x * x, axis=-1, keepdims=True) + EPS) * g).astype(jnp.bfloat16)

    cos_t, sin_a, sin_b = cos_ref[...], sa_ref[...], sb_ref[...]
    half = MLA_ROPE // 2
    q = jnp.dot(norm(cq_ref[...], gq_ref[...]), wuq_ref[...], preferred_element_type=jnp.float32)
    kv = jnp.dot(norm(ckv_ref[...], gkv_ref[...]), wukv_ref[...], preferred_element_type=jnp.float32)
    kpe = _apply_rope(kpe_ref[...].astype(jnp.float32), cos_t, sin_a, sin_b, half)
    for h in range(MLA_HEADS):
        sl = slice(h * LANES, (h + 1) * LANES)
        q_out[:, sl] = (_apply_rope(q[:, sl], cos_t, sin_a, sin_b, half) * scale).astype(q_out.dtype)
        k_out[:, sl] = (kv[:, sl] + kpe).astype(k_out.dtype)
    v_out[...] = kv[:, MLA_HEADS * LANES:].astype(v_out.dtype)


def _mla_prep(proj, g_q, g_kv, w_uq, w_ukv, tables, scale, tm=512):
    t = proj.shape[0]
    nblk = SEQ // tm
    tab_spec = pl.BlockSpec((tm, LANES), lambda i: (i % nblk, 0))
    nq, nk, nv = MLA_HEADS * LANES, MLA_HEADS * LANES, MLA_HEADS * MLA_V
    return pl.pallas_call(
        functools.partial(_mla_prep_kernel, scale=scale),
        grid=(t // tm,),
        in_specs=[pl.BlockSpec((tm, MLA_Q_LORA), lambda i: (i, BLK_CQ * LANES // MLA_Q_LORA)),
                  pl.BlockSpec((tm, MLA_KV_LORA), lambda i: (i, BLK_CKV * LANES // MLA_KV_LORA)),
                  pl.BlockSpec((tm, LANES), lambda i: (i, BLK_KPE)),
                  pl.BlockSpec((1, MLA_Q_LORA), lambda i: (0, 0)),
                  pl.BlockSpec((1, MLA_KV_LORA), lambda i: (0, 0)),
                  pl.BlockSpec(w_uq.shape, lambda i: (0, 0)),
                  pl.BlockSpec(w_ukv.shape, lambda i: (0, 0)),
                  tab_spec, tab_spec, tab_spec],
        out_specs=[pl.BlockSpec((tm, nq), lambda i: (i, 0)),
                   pl.BlockSpec((tm, nk), lambda i: (i, 0)),
                   pl.BlockSpec((tm, nv), lambda i: (i, 0))],
        out_shape=[jax.ShapeDtypeStruct((t, nq), jnp.bfloat16),
                   jax.ShapeDtypeStruct((t, nk), jnp.bfloat16),
                   jax.ShapeDtypeStruct((t, nv), jnp.bfloat16)],
        compiler_params=_cparams(1),
        name="mla_prep",
    )(proj, proj, proj, g_q.reshape(1, -1), g_kv.reshape(1, -1), w_uq, w_ukv, *tables)


def _c_prep_kernel(q_ref, k_ref, gq_ref, gk_ref, cos_ref, sa_ref, sb_ref, q_out, k_out, *, scale):
    lane = lax.broadcasted_iota(jnp.int32, (1, LANES), 1)
    half0 = lane < HEAD_DIM
    cos_t, sin_a, sin_b = cos_ref[...], sa_ref[...], sb_ref[...]

    def prep(x, g):
        x = x.astype(jnp.float32)
        sq = x * x
        s0 = jnp.sum(jnp.where(half0, sq, 0.0), axis=-1, keepdims=True)
        s1 = jnp.sum(jnp.where(half0, 0.0, sq), axis=-1, keepdims=True)
        inv = jnp.where(half0, lax.rsqrt(s0 / HEAD_DIM + EPS), lax.rsqrt(s1 / HEAD_DIM + EPS))
        return _apply_rope(x * inv * g, cos_t, sin_a, sin_b, HEAD_DIM // 4)

    for p in range(q_ref.shape[1] // LANES):
        sl = slice(p * LANES, (p + 1) * LANES)
        q_out[:, sl] = (prep(q_ref[:, sl], gq_ref[...]) * scale).astype(q_out.dtype)
    for p in range(k_ref.shape[1] // LANES):
        sl = slice(p * LANES, (p + 1) * LANES)
        k_out[:, sl] = prep(k_ref[:, sl], gk_ref[...]).astype(k_out.dtype)


def _c_prep(proj, g_q, g_k, tables, scale, tm=512):
    t = proj.shape[0]
    nblk = SEQ // tm
    tab_spec = pl.BlockSpec((tm, LANES), lambda i: (i % nblk, 0))
    qw, kw = N_PAIRS * LANES, C_KV_HEADS * LANES
    g_q2 = jnp.tile(g_q.reshape(1, HEAD_DIM), (1, 2))
    g_k2 = jnp.tile(g_k.reshape(1, HEAD_DIM), (1, 2))
    return pl.pallas_call(
        functools.partial(_c_prep_kernel, scale=scale),
        grid=(t // tm,),
        in_specs=[pl.BlockSpec((tm, qw), lambda i: (i, BLK_QC * LANES // qw)),
                  pl.BlockSpec((tm, kw), lambda i: (i, BLK_KC * LANES // kw)),
                  pl.BlockSpec((1, LANES), lambda i: (0, 0)),
                  pl.BlockSpec((1, LANES), lambda i: (0, 0)),
                  tab_spec, tab_spec, tab_spec],
        out_specs=[pl.BlockSpec((tm, qw), lambda i: (i, 0)), pl.BlockSpec((tm, kw), lambda i: (i, 0))],
        out_shape=[jax.ShapeDtypeStruct((t, qw), jnp.bfloat16), jax.ShapeDtypeStruct((t, kw), jnp.bfloat16)],
        compiler_params=_cparams(1),
        name="axial_prep",
    )(proj, proj, g_q2, g_k2, *tables)


def _dense_kernel(q_ref, k_ref, v_ref, o_ref, *, tq, head_blocks):
    lane = lax.broadcasted_iota(jnp.int32, (1, LANES), 1)
    half0 = lane < HEAD_DIM
    seq = q_ref.shape[0]

    def body(t, carry):
        qs = pl.multiple_of(t * tq, tq)
        q = q_ref[pl.ds(qs, tq), :]
        v = v_ref[...]
        outs = []
        for i in range(2):
            if head_blocks:
                qi = q[:, i * LANES:(i + 1) * LANES]
                ki = k_ref[:, i * LANES:(i + 1) * LANES]
            else:
                qi = jnp.where(half0 if i == 0 else jnp.logical_not(half0), q, jnp.zeros_like(q))
                ki = k_ref[...]
            s = lax.dot_general(qi, ki, (((1,), (1,)), ((), ())), preferred_element_type=jnp.float32)
            m = jnp.max(s, axis=-1, keepdims=True)
            p = jnp.exp(s - m)
            den = jnp.sum(p, axis=-1, keepdims=True)
            outs.append(jnp.dot(p.astype(v.dtype), v, preferred_element_type=jnp.float32) / den)
        o_ref[pl.ds(qs, tq), :] = jnp.where(half0, outs[0], outs[1]).astype(o_ref.dtype)
        return carry

    lax.fori_loop(0, seq // tq, body, 0)


def _dense_attention(q, k, v, q_spec, k_spec, v_spec, head_blocks, name, tq=256):
    t = q.shape[0]
    b = t // SEQ
    return pl.pallas_call(
        functools.partial(_dense_kernel, tq=tq, head_blocks=head_blocks),
        grid=(b, N_PAIRS),
        in_specs=[q_spec, k_spec, v_spec],
        out_specs=pl.BlockSpec((SEQ, LANES), lambda i, j: (i, j)),
        out_shape=jax.ShapeDtypeStruct((t, N_PAIRS * LANES), jnp.bfloat16),
        compiler_params=_cparams(2),
        name=name,
    )(q, k, v)


def _merge_kernel(xn_ref, oa_ref, ob_ref, oc_ref, od_ref, wga_ref, wgb_ref, wgc_ref, wgd_ref, bg_ref, wb_ref,
                  out_ref):
    xn = xn_ref[...]
    acc = None
    branches = ((oa_ref, wga_ref), (ob_ref, wgb_ref), (oc_ref, wgc_ref), (od_ref, wgd_ref))
    for j, (o_ref, wg_ref) in enumerate(branches):
        z = jnp.dot(xn, wg_ref[...], preferred_element_type=jnp.float32) + bg_ref[j:j + 1, :]
        g = 1.0 / (1.0 + jnp.exp(-z))
        u = jnp.dot(o_ref[...], wb_ref[j], preferred_element_type=jnp.float32)
        acc = g * u if acc is None else acc + g * u
    out_ref[...] = acc.astype(out_ref.dtype)


def _merge(xn, o_list, w_all, b_gate, w_branch, layer, tm=512, tc=512):
    t = xn.shape[0]
    nc = D_MODEL // tc
    gate_blk0 = (N_DIL * GROUP_COLS + REST_COLS) // tc
    o_spec = pl.BlockSpec((tm, BRANCH_WIDTH), lambda c, i: (i, 0))

    def wg_spec(j):
        return pl.BlockSpec((None, D_MODEL, tc), lambda c, i, j=j: (layer, 0, gate_blk0 + j * nc + c))

    return pl.pallas_call(
        _merge_kernel,
        grid=(nc, t // tm),
        in_specs=[pl.BlockSpec((tm, D_MODEL), lambda c, i: (i, 0)), o_spec, o_spec, o_spec, o_spec,
                  wg_spec(0), wg_spec(1), wg_spec(2), wg_spec(3),
                  pl.BlockSpec((None, N_BRANCH, tc), lambda c, i: (layer, 0, c)),
                  pl.BlockSpec((None, N_BRANCH, BRANCH_WIDTH, tc), lambda c, i: (layer, 0, 0, c))],
        out_specs=pl.BlockSpec((tm, tc), lambda c, i: (i, c)),
        out_shape=jax.ShapeDtypeStruct((t, D_MODEL), jnp.bfloat16),
        compiler_params=_cparams(2),
        name="gated_merge",
    )(xn, *o_list, w_all, w_all, w_all, w_all, b_gate, w_branch)


def _out_proj_kernel(mixed_ref, w_ref, h_ref, g_ref, wr_hi_ref, wr_lo_ref, h_out, xp_out, probs_out):
    h = h_ref[...] + jnp.dot(mixed_ref[...], w_ref[...], preferred_element_type=jnp.float32)
    h_out[...] = h
    xn = h * lax.rsqrt(jnp.mean(h * h, axis=-1, keepdims=True) + EPS) * g_ref[...]
    half = D_MODEL // 2
    xp_out[...] = pltpu.pack_elementwise([xn[:, :half], xn[:, half:]], packed_dtype=jnp.bfloat16)
    x_hi = xn.astype(jnp.bfloat16)
    x_lo = (xn - x_hi.astype(jnp.float32)).astype(jnp.bfloat16)
    logits = (jnp.dot(x_hi, wr_hi_ref[...], preferred_element_type=jnp.float32)
              + jnp.dot(x_lo, wr_hi_ref[...], preferred_element_type=jnp.float32)
              + jnp.dot(x_hi, wr_lo_ref[...], preferred_element_type=jnp.float32))
    lane = lax.broadcasted_iota(jnp.int32, (1, LANES), 1)
    logits = jnp.where(lane < N_EXPERTS, logits, NEG_INF)
    m = jnp.max(logits, axis=-1, keepdims=True)
    e = jnp.exp(logits - m)
    probs_out[...] = e / jnp.sum(e, axis=-1, keepdims=True)


def _out_proj(mixed, w_out, h, g_ffn, wr_hi, wr_lo, layer, tm=512):
    t = h.shape[0]
    row = lambda w: pl.BlockSpec((tm, w), lambda i: (i, 0))
    lay = lambda a, b: pl.BlockSpec((None, a, b), lambda i: (layer, 0, 0))
    return pl.pallas_call(
        _out_proj_kernel,
        grid=(t // tm,),
        in_specs=[row(D_MODEL), lay(D_MODEL, D_MODEL), row(D_MODEL), lay(1, D_MODEL),
                  lay(D_MODEL, LANES), lay(D_MODEL, LANES)],
        out_specs=[row(D_MODEL), row(D_MODEL // 2), row(LANES)],
        out_shape=[jax.ShapeDtypeStruct((t, D_MODEL), jnp.float32),
                   jax.ShapeDtypeStruct((t, D_MODEL // 2), jnp.uint32),
                   jax.ShapeDtypeStruct((t, LANES), jnp.float32)],
        compiler_params=_cparams(1),
        name="out_proj_router",
    )(mixed, w_out, h, g_ffn, wr_hi, wr_lo)


def _routing_kernel(p_ref, idx_out, gate_out, tri_ref):
    rows, seq = p_ref.shape
    p = p_ref[...]
    tok = lax.broadcasted_iota(jnp.int32, (rows, seq), 1)

    def count(mask):
        return jnp.sum(jnp.where(mask, 1.0, 0.0), axis=-1, keepdims=True)

    def thr_body(i, thr):
        cand = thr | jnp.left_shift(jnp.int32(1), 30 - i)
        cand_f = lax.bitcast_convert_type(cand, jnp.float32)
        return jnp.where(count(p >= cand_f) >= CAPACITY, cand, thr)

    thr = lax.bitcast_convert_type(lax.fori_loop(0, 31, thr_body, jnp.zeros((rows, 1), jnp.int32)), jnp.float32)
    gt = p > thr
    eq = p == thr
    need = CAPACITY - count(gt)

    def tie_body(i, bound):
        cand = bound | jnp.left_shift(jnp.int32(1), 11 - i)
        return jnp.where(count(jnp.logical_and(eq, tok < cand)) < need, cand, bound)

    bound = lax.fori_loop(0, 12, tie_body, jnp.zeros((rows, 1), jnp.int32))
    sel = jnp.logical_or(gt, jnp.logical_and(eq, tok <= bound))

    r_i = lax.broadcasted_iota(jnp.int32, (seq, seq), 0)
    c_i = lax.broadcasted_iota(jnp.int32, (seq, seq), 1)
    tri_ref[...] = jnp.where(r_i < c_i, 1.0, 0.0).astype(jnp.bfloat16)
    self_bf = jnp.where(sel, 1.0, 0.0).astype(jnp.bfloat16)
    pos = jnp.dot(self_bf, tri_ref[...], preferred_element_type=jnp.float32)
    pos = jnp.where(sel, pos, -1.0)

    bf = jnp.bfloat16
    p1 = p.astype(bf).astype(jnp.float32)
    r1 = p - p1
    p2 = r1.astype(bf).astype(jnp.float32)
    p3 = (r1 - p2).astype(bf).astype(jnp.float32)
    t_hi = (tok[0:1] >> 6).astype(jnp.float32)
    t_lo = (tok[0:1] & 63).astype(jnp.float32)
    slot = lax.broadcasted_iota(jnp.int32, (CAPACITY, seq), 0).astype(jnp.float32)
    prow = lax.broadcasted_iota(jnp.int32, (8, seq), 0)
    for r in range(rows):
        onehot = jnp.where(pos[r:r + 1, :] == slot, 1.0, 0.0).astype(bf)
        payload = jnp.where(prow == 0, t_hi, jnp.where(prow == 1, t_lo, jnp.where(
            prow == 2, p1[r:r + 1], jnp.where(prow == 3, p2[r:r + 1], jnp.where(prow == 4, p3[r:r + 1], 0.0)))))
        g = lax.dot_general(payload.astype(bf), onehot, (((1,), (1,)), ((), ())),
                            preferred_element_type=jnp.float32)
        idx_out[r:r + 1, :] = (g[0:1] * 64.0 + g[1:2]).astype(jnp.int32)
        gate_out[r:r + 1, :] = g[2:3] + g[3:4] + g[4:5]


def _routing(probs_t):
    rows, seq = probs_t.shape
    rt = N_EXPERTS
    return pl.pallas_call(
        _routing_kernel,
        grid=(rows // rt,),
        in_specs=[pl.BlockSpec((rt, seq), lambda i: (i, 0))],
        out_specs=[pl.BlockSpec((rt, CAPACITY), lambda i: (i, 0)), pl.BlockSpec((rt, CAPACITY), lambda i: (i, 0))],
        out_shape=[jax.ShapeDtypeStruct((rows, CAPACITY), jnp.int32),
                   jax.ShapeDtypeStruct((rows, CAPACITY), jnp.float32)],
        scratch_shapes=[pltpu.VMEM((seq, seq), jnp.bfloat16)],
        compiler_params=_cparams(1),
        name="expert_choice_routing",
    )(probs_t)


def _gather_kernel(idx_ref, x_ref, o_ref):
    def body(i, c):
        t = idx_ref[0, i]
        o_ref[pl.ds(i, 1), :] = x_ref[pl.ds(t, 1), :]
        return c
    lax.fori_loop(0, CAPACITY, body, 0, unroll=8)


def _gather(idx, x_packed):
    b, e = idx.shape[:2]
    w = x_packed.shape[-1]
    return pl.pallas_call(
        _gather_kernel,
        grid=(b, e),
        in_specs=[pl.BlockSpec((None, None, 1, CAPACITY), lambda i, j: (i, j, 0, 0), memory_space=pltpu.SMEM),
                  pl.BlockSpec((None, SEQ, w), lambda i, j: (i, 0, 0))],
        out_specs=pl.BlockSpec((None, None, CAPACITY, w), lambda i, j: (i, j, 0, 0)),
        out_shape=jax.ShapeDtypeStruct((b, e, CAPACITY, w), jnp.uint32),
        compiler_params=_cparams(2),
        name="moe_gather",
    )(idx, x_packed)


def _ffn_kernel(x_ref, wg_ref, wu_ref, wd_ref, o_ref):
    xp = x_ref[...]
    bf = jnp.bfloat16
    lo = pltpu.unpack_elementwise(xp, index=0, packed_dtype=bf, unpacked_dtype=jnp.float32).astype(bf)
    hi = pltpu.unpack_elementwise(xp, index=1, packed_dtype=bf, unpacked_dtype=jnp.float32).astype(bf)
    x = jnp.concatenate([lo, hi], axis=1)
    a = jnp.dot(x, wg_ref[...], preferred_element_type=jnp.float32)
    u = jnp.dot(x, wu_ref[...], preferred_element_type=jnp.float32)
    h = (a / (1.0 + jnp.exp(-a)) * u).astype(bf)
    o_ref[...] = jnp.dot(h, wd_ref[...], preferred_element_type=jnp.float32).astype(o_ref.dtype)


def _ffn(xe, w_gate, w_up, w_down, layer):
    b, e, c, w = xe.shape
    return pl.pallas_call(
        _ffn_kernel,
        grid=(e, b),
        in_specs=[pl.BlockSpec((None, None, c, w), lambda j, i: (i, j, 0, 0)),
                  pl.BlockSpec((None, None, D_MODEL, EXPERT_FF), lambda j, i: (layer, j, 0, 0)),
                  pl.BlockSpec((None, None, D_MODEL, EXPERT_FF), lambda j, i: (layer, j, 0, 0)),
                  pl.BlockSpec((None, None, EXPERT_FF, D_MODEL), lambda j, i: (layer, j, 0, 0))],
        out_specs=pl.BlockSpec((None, None, c, D_MODEL), lambda j, i: (i, j, 0, 0)),
        out_shape=jax.ShapeDtypeStruct((b, e, c, D_MODEL), jnp.bfloat16),
        compiler_params=_cparams(2),
        name="expert_ffn",
    )(xe, w_gate, w_up, w_down)


COMBINE_ROWS = 8


def _combine_kernel(idx_ref, gate_ref, y_ref, h_ref, o_ref, ybuf):
    @pl.when(pl.program_id(2) == 0)
    def _():
        o_ref[...] = h_ref[...]

    ybuf[...] = y_ref[...].astype(jnp.float32)

    def body(c, carry):
        base = c * COMBINE_ROWS
        toks = [idx_ref[0, base + j] for j in range(COMBINE_ROWS)]
        new = [o_ref[pl.ds(toks[j], 1), :] + gate_ref[0, base + j] * ybuf[pl.ds(base + j, 1), :]
               for j in range(COMBINE_ROWS)]
        for j in range(COMBINE_ROWS):
            o_ref[pl.ds(toks[j], 1), :] = new[j]
        return carry

    lax.fori_loop(0, CAPACITY // COMBINE_ROWS, body, 0)


def _combine(idx, gate, ye, h, n_split=2):
    b, e = idx.shape[:2]
    wd = D_MODEL // n_split
    smem = lambda: pl.BlockSpec((None, None, 1, CAPACITY), lambda i, s, j: (i, j, 0, 0), memory_space=pltpu.SMEM)
    return pl.pallas_call(
        _combine_kernel,
        grid=(b, n_split, e),
        in_specs=[smem(), smem(),
                  pl.BlockSpec((None, None, CAPACITY, wd), lambda i, s, j: (i, j, 0, s)),
                  pl.BlockSpec((None, SEQ, wd), lambda i, s, j: (i, 0, s))],
        out_specs=pl.BlockSpec((None, SEQ, wd), lambda i, s, j: (i, 0, s)),
        out_shape=jax.ShapeDtypeStruct((b, SEQ, D_MODEL), jnp.float32),
        scratch_shapes=[pltpu.VMEM((CAPACITY, wd), jnp.float32)],
        compiler_params=_cparams(3),
        name="moe_combine",
    )(idx, gate, ye, h)


def _alibi_slopes(n):
    return jnp.exp2(-8.0 * jnp.arange(1, n + 1, dtype=jnp.float32) / n)


def _token_mixer(h, l, p, tabs):
    t = h.shape[0]
    b = t // SEQ
    bf = jnp.bfloat16
    xn = _rmsnorm(h, p["g_attn_norm"][l], bf)
    groups = _proj_dilated(xn, p["w_all"], l)
    proj = _proj_rest(xn, p["w_all"], l, tm=min(t, 2048))
    proj4 = proj.reshape(b, 1, SEQ, REST_COLS)

    slopes = _alibi_slopes(A_HEADS)
    outs, lses = [], []
    for g, (window, dil) in enumerate(A_DILATIONS):
        o, lse = _banded(groups[g], lambda j: (j, N_PAIRS + j, 2 * N_PAIRS + j), slopes * dil, slopes,
                         (window // 2) // dil, False, True, jnp.float32, "banded_dilated")
        outs.append(o)
        lses.append(lse)
    o_a = _a_combine(outs, lses)

    qb, kb, vb = _mla_prep(proj, p["g_mla_q"][l], p["g_mla_kv"][l], p["w_mla_uq"][l], p["w_mla_ukv"][l],
                           tabs["mla"], (MLA_NOPE + MLA_ROPE) ** -0.5)
    o_b = _dense_attention(qb, kb, vb,
                           pl.BlockSpec((SEQ, 2 * LANES), lambda i, j: (i, j)),
                           pl.BlockSpec((SEQ, 2 * LANES), lambda i, j: (i, j)),
                           pl.BlockSpec((SEQ, LANES), lambda i, j: (i, j)), True, "latent_attention")

    qc, kc = _c_prep(proj, p["g_c_q"][l], p["g_c_k"][l], tabs["axial"], HEAD_DIM ** -0.5)
    o_c = _dense_attention(qc, kc, proj,
                           pl.BlockSpec((SEQ, LANES), lambda i, j: (i, j)),
                           pl.BlockSpec((SEQ, LANES), lambda i, j: (i, j // 2)),
                           pl.BlockSpec((SEQ, LANES), lambda i, j: (i, BLK_VC + j // 2)), False, "axial_attention")

    (o_d,) = _banded(proj4, lambda j: (BLK_QD + j, BLK_KD + j // 2, BLK_VD + j // 2),
                     _alibi_slopes(D_Q_HEADS), p["sink_d"][l].astype(jnp.float32),
                     D_HALF_WINDOW, True, False, bf, "banded_sink")

    return _merge(xn, (o_a, o_b, o_c, o_d), p["w_all"], p["b_gate"], p["w_branch"], l)


def _layer(h, l, p, tabs):
    t = h.shape[0]
    b = t // SEQ
    mixed = _token_mixer(h, l, p, tabs)
    h, x_packed, probs = _out_proj(mixed, p["w_out"], h, p["g_ffn_norm"], p["wr_hi"], p["wr_lo"], l)
    probs_t = jnp.transpose(probs.reshape(b, SEQ, LANES)[:, :, :N_EXPERTS], (0, 2, 1)).reshape(b * N_EXPERTS, SEQ)
    idx, gate = _routing(probs_t)
    idx = idx.reshape(b, N_EXPERTS, 1, CAPACITY)
    gate = gate.reshape(b, N_EXPERTS, 1, CAPACITY)
    xe = _gather(idx, x_packed.reshape(b, SEQ, D_MODEL // 2))
    ye = _ffn(xe, p["w_exp_gate"], p["w_exp_up"], p["w_exp_down"], l)
    return _combine(idx, gate, ye, h.reshape(b, SEQ, D_MODEL)).reshape(t, D_MODEL)


def _prepare_params(w_in, w_mla_uq, w_mla_ukv, w_branch, w_out, w_router, w_exp_gate, w_exp_up, w_exp_down):
    bf = jnp.bfloat16
    p = {}
    p["w_all"] = _relayout_w_in(w_in)
    p["w_mla_uq"] = jnp.stack([_mla_uq_layout(w_mla_uq[l]) for l in range(DEPTH)]).astype(bf)
    p["w_mla_ukv"] = jnp.stack([_mla_ukv_layout(w_mla_ukv[l]) for l in range(DEPTH)]).astype(bf)
    p["w_branch"] = w_branch.astype(bf)
    p["w_out"] = w_out.astype(bf)
    wr = jnp.pad(w_router, ((0, 0), (0, 0), (0, LANES - N_EXPERTS)))
    p["wr_hi"] = wr.astype(bf)
    p["wr_lo"] = (wr - p["wr_hi"].astype(jnp.float32)).astype(bf)
    p["w_exp_gate"] = w_exp_gate.astype(bf)
    p["w_exp_up"] = w_exp_up.astype(bf)
    p["w_exp_down"] = w_exp_down.astype(bf)
    return p


def kernel(x, w_in, b_gate, g_attn_norm, g_ffn_norm, g_mla_q, g_mla_kv, w_mla_uq, w_mla_ukv, g_c_q, g_c_k, sink_d,
           w_branch, w_out, w_router, w_exp_gate, w_exp_up, w_exp_down, g_final):
    b, s, d = x.shape
    assert (s, d) == (SEQ, D_MODEL)
    p = _prepare_params(w_in, w_mla_uq, w_mla_ukv, w_branch, w_out, w_router, w_exp_gate, w_exp_up, w_exp_down)
    p.update(b_gate=b_gate, g_attn_norm=g_attn_norm, g_ffn_norm=g_ffn_norm.reshape(DEPTH, 1, D_MODEL),
             g_mla_q=g_mla_q, g_mla_kv=g_mla_kv, g_c_q=g_c_q, g_c_k=g_c_k, sink_d=sink_d)
    pos = jnp.arange(SEQ)
    rows, cols = pos // GRID_W, pos % GRID_W
    tabs = {
        "mla": _rope_tables([pos], MLA_ROPE, [MLA_NOPE]),
        "axial": _rope_tables([rows, cols, rows, cols], HEAD_DIM // 2, [0, 32, 64, 96]),
    }
    h = x.reshape(b * s, d)
    for l in range(DEPTH):
        h = _layer(h, l, p, tabs)
    return _rmsnorm(h, g_final, jnp.float32).reshape(b, s, d)
```

```python
import functools

import jax
import jax.numpy as jnp
from jax import lax
from jax.experimental import pallas as pl
from jax.experimental.pallas import tpu as pltpu

D_MODEL = 2048
SEQ = 2048
DEPTH = 2
HEAD_DIM = 64
GRID_W = 64
ROPE_THETA = 10000.0
NEG_INF = -1e30
EPS = 1e-6
N_BRANCH = 4
BRANCH_WIDTH = 512

A_HEADS = 8
A_DILATIONS = ((128, 1), (512, 4), (2048, 16))
N_DIL = 3
MLA_HEADS = 8
MLA_Q_LORA = 512
MLA_KV_LORA = 256
MLA_NOPE = 64
MLA_ROPE = 32
MLA_V = 64
C_Q_HEADS = 8
C_KV_HEADS = 2
D_Q_HEADS = 8
D_KV_HEADS = 2
D_HALF_WINDOW = 128

A_COLS = 3 * N_DIL * A_HEADS * HEAD_DIM
B_COLS = MLA_Q_LORA + MLA_KV_LORA + MLA_ROPE
C_COLS = (C_Q_HEADS + 2 * C_KV_HEADS) * HEAD_DIM
D_COLS = (D_Q_HEADS + 2 * D_KV_HEADS) * HEAD_DIM
GATE_COL0 = A_COLS + B_COLS + C_COLS + D_COLS

N_EXPERTS = 16
EXPERT_FF = D_MODEL // 2
CAPACITY = 2 * SEQ // N_EXPERTS

LANES = 128
N_PAIRS = 4
DEN_LANE = HEAD_DIM
VMEM_LIMIT = 56 * 1024 * 1024

GROUP_COLS = 3 * A_HEADS * HEAD_DIM
REST_COLS = 3072
GATE_COLS = N_BRANCH * D_MODEL
BLK_CQ = 0
BLK_QC = 4
BLK_QD = 8
BLK_CKV = 12
BLK_KPE = 14
BLK_KC = 16
BLK_VC = 18
BLK_KD = 20
BLK_VD = 22


def _cparams(n_grid_dims):
    return pltpu.CompilerParams(dimension_semantics=("arbitrary",) * n_grid_dims, vmem_limit_bytes=VMEM_LIMIT)


def _proj_segments():
    segs = []
    for g in range(N_DIL):
        for s in range(3):
            segs.append(((s * N_DIL + g) * A_HEADS * HEAD_DIM, A_HEADS * HEAD_DIM))
    b0 = A_COLS
    c0 = A_COLS + B_COLS
    d0 = c0 + C_COLS
    segs.append((b0, MLA_Q_LORA))
    segs.append((c0, C_Q_HEADS * HEAD_DIM))
    segs.append((d0, D_Q_HEADS * HEAD_DIM))
    segs.append((b0 + MLA_Q_LORA, MLA_KV_LORA))
    segs += [(None, 64), (b0 + MLA_Q_LORA + MLA_KV_LORA, MLA_ROPE), (None, 32 + LANES)]
    for base, nq, nkv in ((c0, C_Q_HEADS, C_KV_HEADS), (d0, D_Q_HEADS, D_KV_HEADS)):
        for part in range(2):
            for kv in range(nkv):
                start = base + (nq + part * nkv + kv) * HEAD_DIM
                segs += [(start, HEAD_DIM), (start, HEAD_DIM)]
    assert sum(w for _, w in segs) == N_DIL * GROUP_COLS + REST_COLS
    segs.append((GATE_COL0, GATE_COLS))
    return segs


def _relayout_w_in(w_in):
    wt = jnp.transpose(w_in, (0, 2, 1))
    parts = []
    for start, width in _proj_segments():
        if start is None:
            parts.append(jnp.zeros((wt.shape[0], width, wt.shape[2]), jnp.bfloat16))
        else:
            parts.append(wt[:, start:start + width, :].astype(jnp.bfloat16))
    return jnp.concatenate(parts, axis=1)


def _dot_nt(x, w):
    return lax.dot_general(x, w, (((1,), (1,)), ((), ())), preferred_element_type=jnp.float32)


def _mla_uq_layout(w_uq):
    w = w_uq.reshape(MLA_Q_LORA, MLA_HEADS, MLA_NOPE + MLA_ROPE)
    w = jnp.pad(w, ((0, 0), (0, 0), (0, LANES - MLA_NOPE - MLA_ROPE)))
    return w.reshape(MLA_Q_LORA, MLA_HEADS * LANES)


def _mla_ukv_layout(w_ukv):
    w = w_ukv.reshape(MLA_KV_LORA, MLA_HEADS, MLA_NOPE + MLA_V)
    wk = jnp.pad(w[:, :, :MLA_NOPE], ((0, 0), (0, 0), (0, LANES - MLA_NOPE)))
    wv = jnp.pad(w[:, :, MLA_NOPE:], ((0, 0), (0, 0), (0, LANES - MLA_V)))
    return jnp.concatenate([wk.reshape(MLA_KV_LORA, MLA_HEADS * LANES),
                            wv.reshape(MLA_KV_LORA, MLA_HEADS * LANES)], axis=1)


def _rope_tables(pos_list, dim, lane_offsets):
    half = dim // 2
    freqs = ROPE_THETA ** (-jnp.arange(0, dim, 2, dtype=jnp.float32) / dim)
    cos_t = jnp.ones((SEQ, LANES), jnp.float32)
    sin_a = jnp.zeros((SEQ, LANES), jnp.float32)
    sin_b = jnp.zeros((SEQ, LANES), jnp.float32)
    for pos, off in zip(pos_list, lane_offsets):
        ang = pos.astype(jnp.float32)[:, None] * freqs[None, :]
        c, s = jnp.cos(ang), jnp.sin(ang)
        cos_t = cos_t.at[:, off:off + half].set(c).at[:, off + half:off + dim].set(c)
        sin_a = sin_a.at[:, off:off + half].set(-s)
        sin_b = sin_b.at[:, off + half:off + dim].set(s)
    return cos_t, sin_a, sin_b


def _apply_rope(x, cos_t, sin_a, sin_b, half):
    n = x.shape[-1]
    return (x * cos_t + pltpu.roll(x, n - half, 1) * sin_a + pltpu.roll(x, half, 1) * sin_b)


def _rmsnorm_kernel(x_ref, g_ref, o_ref):
    x = x_ref[...]
    ms = jnp.mean(x * x, axis=-1, keepdims=True)
    o_ref[...] = (x * lax.rsqrt(ms + EPS) * g_ref[...]).astype(o_ref.dtype)


def _rmsnorm(x, g, out_dtype, tm=512):
    t, d = x.shape
    return pl.pallas_call(
        _rmsnorm_kernel,
        grid=(t // tm,),
        in_specs=[pl.BlockSpec((tm, d), lambda i: (i, 0)), pl.BlockSpec((1, d), lambda i: (0, 0))],
        out_specs=pl.BlockSpec((tm, d), lambda i: (i, 0)),
        out_shape=jax.ShapeDtypeStruct((t, d), out_dtype),
        compiler_params=_cparams(1),
        name="rmsnorm",
    )(x, g.reshape(1, d))


def _matmul_kernel(x_ref, w_ref, o_ref):
    o_ref[...] = _dot_nt(x_ref[...], w_ref[...]).astype(o_ref.dtype)


def _proj_rest(x, w_all, layer, tm, tn=1536):
    m, k = x.shape
    blk0 = N_DIL * GROUP_COLS // tn
    return pl.pallas_call(
        _matmul_kernel,
        grid=(m // tm, REST_COLS // tn),
        in_specs=[pl.BlockSpec((tm, k), lambda i, j: (i, 0)),
                  pl.BlockSpec((None, tn, k), lambda i, j: (layer, blk0 + j, 0))],
        out_specs=pl.BlockSpec((tm, tn), lambda i, j: (i, j)),
        out_shape=jax.ShapeDtypeStruct((m, REST_COLS), jnp.bfloat16),
        compiler_params=_cparams(2),
        name="in_proj",
    )(x, w_all)


def _proj_dilated_kernel(x_ref, w0_ref, w1_ref, w2_ref, o0_ref, o1_ref, o2_ref, scr_ref):
    x = x_ref[...]
    for w_ref, o_ref, (_, dil) in zip((w0_ref, w1_ref, w2_ref), (o0_ref, o1_ref, o2_ref), A_DILATIONS):
        acc = _dot_nt(x, w_ref[...])
        if dil == 1:
            o_ref[0] = acc.astype(o_ref.dtype)
        else:
            n_blk = acc.shape[1] // LANES
            for c in range(n_blk):
                scr_ref[c] = acc[:, c * LANES:(c + 1) * LANES]
            for r in range(dil):
                for c in range(n_blk):
                    rows = scr_ref[c, pl.ds(r, SEQ // dil, stride=dil), :]
                    o_ref[r, :, c * LANES:(c + 1) * LANES] = rows.astype(o_ref.dtype)


def _proj_dilated(x, w_all, layer, tn=512):
    t, k = x.shape
    b = t // SEQ
    nc = GROUP_COLS // tn

    def w_spec(g):
        return pl.BlockSpec((None, tn, k), lambda i, c, g=g: (layer, g * nc + c, 0))

    out_specs, out_shape = [], []
    for _, dil in A_DILATIONS:
        ld = SEQ // dil
        out_specs.append(pl.BlockSpec((None, dil, ld, tn), lambda i, c: (i, 0, 0, c)))
        out_shape.append(jax.ShapeDtypeStruct((b, dil, ld, GROUP_COLS), jnp.bfloat16))
    return pl.pallas_call(
        _proj_dilated_kernel,
        grid=(b, nc),
        in_specs=[pl.BlockSpec((SEQ, k), lambda i, c: (i, 0)), w_spec(0), w_spec(1), w_spec(2)],
        out_specs=out_specs,
        out_shape=out_shape,
        scratch_shapes=[pltpu.VMEM((tn // LANES, SEQ, LANES), jnp.float32)],
        compiler_params=_cparams(2),
        name="in_proj_dilated",
    )(x, w_all, w_all, w_all)


BANDED_GROUP_SCORE_ELEMS = 512 * 1024


def _banded_kernel(slope_ref, sink_ref, q_ref, k_ref, v_ref, *refs, length, half_window, dil, use_sink, want_lse):
    o_ref = refs[0]
    lse_ref = refs[1] if want_lse else None
    bias_ref = refs[-1]
    hp = pl.program_id(1)
    w = half_window
    tq = 2 * w
    tk = min(4 * w, length)
    nq = length // tq
    key_start = [min(max(qb * tq - w, 0), length - tk) for qb in range(nq)]
    offsets = sorted({key_start[qb] - qb * tq for qb in range(nq)})

    lane = lax.broadcasted_iota(jnp.int32, (1, LANES), 1)
    half0 = lane < HEAD_DIM
    delta = lax.broadcasted_iota(jnp.int32, (tq, tk), 1) - lax.broadcasted_iota(jnp.int32, (tq, tk), 0)
    slopes = [slope_ref[2 * hp + i] for i in range(2)]
    for oi, off in enumerate(offsets):
        rel = jnp.abs(delta + off)
        dist = rel.astype(jnp.float32)
        for i in range(2):
            bias_ref[2 * oi + i] = jnp.where(rel <= w, -slopes[i] * dist, NEG_INF)

    scale = jnp.asarray(HEAD_DIM ** -0.5, q_ref.dtype)
    blocks = [(r, qb) for r in range(dil) for qb in range(nq)]
    group = max(1, BANDED_GROUP_SCORE_ELEMS // (2 * tq * tk))
    for g0 in range(0, len(blocks), group):
        grp = blocks[g0:g0 + group]
        scores = []
        for r, qb in grp:
            qs, ks = qb * tq, key_start[qb]
            oi = offsets.index(ks - qs)
            q = q_ref[r, qs:qs + tq, :] * scale
            kw = k_ref[r, ks:ks + tk, :]
            for i in range(2):
                qi = jnp.where(half0 if i == 0 else jnp.logical_not(half0), q, jnp.zeros_like(q))
                scores.append(_dot_nt(qi, kw) + bias_ref[2 * oi + i])
        probs, maxes, dens = [], [], []
        for n, s in enumerate(scores):
            m = jnp.max(s, axis=-1, keepdims=True)
            if use_sink:
                sk = sink_ref[2 * hp + n % 2]
                m = jnp.maximum(m, sk)
            p = jnp.exp(s - m)
            den = jnp.sum(p, axis=-1, keepdims=True)
            if use_sink:
                den = den + jnp.exp(sk - m)
            probs.append(p.astype(v_ref.dtype))
            maxes.append(m)
            dens.append(den)
        for n, (r, qb) in enumerate(grp):
            qs, ks = qb * tq, key_start[qb]
            vw = v_ref[r, ks:ks + tk, :]
            outs = [jnp.dot(probs[2 * n + i], vw, preferred_element_type=jnp.float32) / dens[2 * n + i]
                    for i in range(2)]
            rows = pl.ds(qs, tq) if dil == 1 else pl.ds(r + qs * dil, tq, stride=dil)
            o_ref[rows, :] = jnp.where(half0, outs[0], outs[1]).astype(o_ref.dtype)
            if want_lse:
                lses = [maxes[2 * n + i] + jnp.log(dens[2 * n + i]) for i in range(2)]
                lse_ref[rows, :] = jnp.where(half0, lses[0], lses[1]).astype(lse_ref.dtype)


def _banded(qkv, blocks, slopes, sinks, half_window, use_sink, want_lse, out_dtype, name):
    b, dil, length, _ = qkv.shape
    tq = 2 * half_window
    tk = min(4 * half_window, length)
    n_off = 1 if tk == length else 3
    kern = functools.partial(_banded_kernel, length=length, half_window=half_window, dil=dil,
                             use_sink=use_sink, want_lse=want_lse)
    smem = pl.BlockSpec(memory_space=pltpu.SMEM)

    def spec(s):
        return pl.BlockSpec((None, dil, length, LANES), lambda i, j, s=s: (i, 0, 0, blocks(j)[s]))

    out_spec = pl.BlockSpec((SEQ, LANES), lambda i, j: (i, j))
    out_specs = [out_spec]
    out_shape = [jax.ShapeDtypeStruct((b * SEQ, N_PAIRS * LANES), out_dtype)]
    if want_lse:
        out_specs.append(out_spec)
        out_shape.append(jax.ShapeDtypeStruct((b * SEQ, N_PAIRS * LANES), jnp.float32))
    return pl.pallas_call(
        kern,
        grid=(b, N_PAIRS),
        in_specs=[smem, smem, spec(0), spec(1), spec(2)],
        out_specs=out_specs,
        out_shape=out_shape,
        scratch_shapes=[pltpu.VMEM((2 * n_off, tq, tk), jnp.float32)],
        compiler_params=_cparams(2),
        name=name,
    )(slopes, sinks, qkv, qkv, qkv)


def _a_combine_kernel(o0, o1, o2, l0, l1, l2, out_ref):
    la, lb, lc = l0[...], l1[...], l2[...]
    m = jnp.maximum(jnp.maximum(la, lb), lc)
    ea, eb, ec = jnp.exp(la - m), jnp.exp(lb - m), jnp.exp(lc - m)
    num = ea * o0[...] + eb * o1[...] + ec * o2[...]
    out_ref[...] = (num / (ea + eb + ec)).astype(out_ref.dtype)


def _a_combine(outs, lses, tm=1024):
    t, w = outs[0].shape
    spec = pl.BlockSpec((tm, w), lambda i: (i, 0))
    return pl.pallas_call(
        _a_combine_kernel,
        grid=(t // tm,),
        in_specs=[spec] * 6,
        out_specs=spec,
        out_shape=jax.ShapeDtypeStruct((t, w), jnp.bfloat16),
        compiler_params=_cparams(1),
        name="dilated_combine",
    )(*outs, *lses)


def _mla_prep_kernel(cq_ref, ckv_ref, kpe_ref, gq_ref, gkv_ref, wuq_ref, wukv_ref, cos_ref, sa_ref, sb_ref,
                     q_out, k_out, v_out, *, scale):
    def norm(x, g):
        x = x.astype(jnp.float32)
        return (x * lax.rsqrt(jnp.mean(x * x, axis=-1, keepdims=True) + EPS) * g).astype(jnp.bfloat16)

    cos_t, sin_a, sin_b = cos_ref[...], sa_ref[...], sb_ref[...]
    half = MLA_ROPE // 2
    q = jnp.dot(norm(cq_ref[...], gq_ref[...]), wuq_ref[...], preferred_element_type=jnp.float32)
    kv = jnp.dot(norm(ckv_ref[...], gkv_ref[...]), wukv_ref[...], preferred_element_type=jnp.float32)
    kpe = _apply_rope(kpe_ref[...].astype(jnp.float32), cos_t, sin_a, sin_b, half)
    for h in range(MLA_HEADS):
        sl = slice(h * LANES, (h + 1) * LANES)
        q_out[:, sl] = (_apply_rope(q[:, sl], cos_t, sin_a, sin_b, half) * scale).astype(q_out.dtype)
        k_out[:, sl] = (kv[:, sl] + kpe).astype(k_out.dtype)
    lane = lax.broadcasted_iota(jnp.int32, (1, MLA_HEADS * LANES), 1)
    ones_lane = jnp.where(lane % LANES == DEN_LANE, 1.0, 0.0)
    v_out[...] = (kv[:, MLA_HEADS * LANES:] + ones_lane).astype(v_out.dtype)


def _mla_prep(proj, g_q, g_kv, w_uq, w_ukv, tables, scale, tm=512):
    t = proj.shape[0]
    nblk = SEQ // tm
    tab_spec = pl.BlockSpec((tm, LANES), lambda i: (i % nblk, 0))
    nq, nk, nv = MLA_HEADS * LANES, MLA_HEADS * LANES, MLA_HEADS * LANES
    return pl.pallas_call(
        functools.partial(_mla_prep_kernel, scale=scale),
        grid=(t // tm,),
        in_specs=[pl.BlockSpec((tm, MLA_Q_LORA), lambda i: (i, BLK_CQ * LANES // MLA_Q_LORA)),
                  pl.BlockSpec((tm, MLA_KV_LORA), lambda i: (i, BLK_CKV * LANES // MLA_KV_LORA)),
                  pl.BlockSpec((tm, LANES), lambda i: (i, BLK_KPE)),
                  pl.BlockSpec((1, MLA_Q_LORA), lambda i: (0, 0)),
                  pl.BlockSpec((1, MLA_KV_LORA), lambda i: (0, 0)),
                  pl.BlockSpec(w_uq.shape, lambda i: (0, 0)),
                  pl.BlockSpec(w_ukv.shape, lambda i: (0, 0)),
                  tab_spec, tab_spec, tab_spec],
        out_specs=[pl.BlockSpec((tm, nq), lambda i: (i, 0)),
                   pl.BlockSpec((tm, nk), lambda i: (i, 0)),
                   pl.BlockSpec((tm, nv), lambda i: (i, 0))],
        out_shape=[jax.ShapeDtypeStruct((t, nq), jnp.bfloat16),
                   jax.ShapeDtypeStruct((t, nk), jnp.bfloat16),
                   jax.ShapeDtypeStruct((t, nv), jnp.bfloat16)],
        compiler_params=_cparams(1),
        name="mla_prep",
    )(proj, proj, proj, g_q.reshape(1, -1), g_kv.reshape(1, -1), w_uq, w_ukv, *tables)


def _c_prep_kernel(q_ref, k_ref, v_ref, gq_ref, gk_ref, cos_ref, sa_ref, sb_ref, q_out, k_out, v_out, *, scale):
    lane = lax.broadcasted_iota(jnp.int32, (1, LANES), 1)
    half0 = lane < HEAD_DIM
    cos_t, sin_a, sin_b = cos_ref[...], sa_ref[...], sb_ref[...]
    ones_lane = jnp.where(lane == DEN_LANE, 1.0, 0.0).astype(v_out.dtype)
    for p in range(v_ref.shape[1] // LANES):
        sl = slice(p * LANES, (p + 1) * LANES)
        v_out[:, sl] = jnp.where(half0, v_ref[:, sl], ones_lane)

    def prep(x, g):
        x = x.astype(jnp.float32)
        sq = x * x
        s0 = jnp.sum(jnp.where(half0, sq, 0.0), axis=-1, keepdims=True)
        s1 = jnp.sum(jnp.where(half0, 0.0, sq), axis=-1, keepdims=True)
        inv = jnp.where(half0, lax.rsqrt(s0 / HEAD_DIM + EPS), lax.rsqrt(s1 / HEAD_DIM + EPS))
        return _apply_rope(x * inv * g, cos_t, sin_a, sin_b, HEAD_DIM // 4)

    for p in range(q_ref.shape[1] // LANES):
        sl = slice(p * LANES, (p + 1) * LANES)
        q_out[:, sl] = (prep(q_ref[:, sl], gq_ref[...]) * scale).astype(q_out.dtype)
    for p in range(k_ref.shape[1] // LANES):
        sl = slice(p * LANES, (p + 1) * LANES)
        k_out[:, sl] = prep(k_ref[:, sl], gk_ref[...]).astype(k_out.dtype)


def _c_prep(proj, g_q, g_k, tables, scale, tm=512):
    t = proj.shape[0]
    nblk = SEQ // tm
    tab_spec = pl.BlockSpec((tm, LANES), lambda i: (i % nblk, 0))
    qw, kw = N_PAIRS * LANES, C_KV_HEADS * LANES
    g_q2 = jnp.tile(g_q.reshape(1, HEAD_DIM), (1, 2))
    g_k2 = jnp.tile(g_k.reshape(1, HEAD_DIM), (1, 2))
    return pl.pallas_call(
        functools.partial(_c_prep_kernel, scale=scale),
        grid=(t // tm,),
        in_specs=[pl.BlockSpec((tm, qw), lambda i: (i, BLK_QC * LANES // qw)),
                  pl.BlockSpec((tm, kw), lambda i: (i, BLK_KC * LANES // kw)),
                  pl.BlockSpec((tm, kw), lambda i: (i, BLK_VC * LANES // kw)),
                  pl.BlockSpec((1, LANES), lambda i: (0, 0)),
                  pl.BlockSpec((1, LANES), lambda i: (0, 0)),
                  tab_spec, tab_spec, tab_spec],
        out_specs=[pl.BlockSpec((tm, qw), lambda i: (i, 0)), pl.BlockSpec((tm, kw), lambda i: (i, 0)),
                   pl.BlockSpec((tm, kw), lambda i: (i, 0))],
        out_shape=[jax.ShapeDtypeStruct((t, qw), jnp.bfloat16), jax.ShapeDtypeStruct((t, kw), jnp.bfloat16),
                   jax.ShapeDtypeStruct((t, kw), jnp.bfloat16)],
        compiler_params=_cparams(1),
        name="axial_prep",
    )(proj, proj, proj, g_q2, g_k2, *tables)


DENSE_TILES_PER_STEP = 2


def _dense_kernel(q_ref, k_ref, v_ref, o_ref, *, tq, head_blocks):
    lane = lax.broadcasted_iota(jnp.int32, (1, LANES), 1)
    half0 = lane < HEAD_DIM
    seq = q_ref.shape[0]

    def head_operand(ref, i):
        return ref[:, i * LANES:(i + 1) * LANES] if head_blocks else ref[...]

    def body(t, carry):
        starts = [pl.multiple_of((t * DENSE_TILES_PER_STEP + u) * tq, tq) for u in range(DENSE_TILES_PER_STEP)]
        scores = []
        for qs in starts:
            q = q_ref[pl.ds(qs, tq), :]
            for i in range(2):
                if head_blocks:
                    qi = q[:, i * LANES:(i + 1) * LANES]
                else:
                    qi = jnp.where(half0 if i == 0 else jnp.logical_not(half0), q, jnp.zeros_like(q))
                scores.append(_dot_nt(qi, head_operand(k_ref, i)))
        for u, qs in enumerate(starts):
            outs = []
            for i in range(2):
                s = scores[2 * u + i]
                vi = head_operand(v_ref, i)
                m = jnp.max(s, axis=-1, keepdims=True)
                p = jnp.exp((s - m).astype(vi.dtype))
                pv = jnp.dot(p, vi, preferred_element_type=jnp.float32)
                outs.append(pv * (1.0 / pv[:, DEN_LANE:DEN_LANE + 1]))
            o_ref[pl.ds(qs, tq), :] = jnp.where(half0, outs[0], pltpu.roll(outs[1], HEAD_DIM, 1)).astype(o_ref.dtype)
        return carry

    lax.fori_loop(0, seq // (tq * DENSE_TILES_PER_STEP), body, 0)


def _dense_attention(q, k, v, q_spec, k_spec, v_spec, head_blocks, name, tq=256):
    t = q.shape[0]
    b = t // SEQ
    return pl.pallas_call(
        functools.partial(_dense_kernel, tq=tq, head_blocks=head_blocks),
        grid=(b, N_PAIRS),
        in_specs=[q_spec, k_spec, v_spec],
        out_specs=pl.BlockSpec((SEQ, LANES), lambda i, j: (i, j)),
        out_shape=jax.ShapeDtypeStruct((t, N_PAIRS * LANES), jnp.bfloat16),
        compiler_params=_cparams(2),
        name=name,
    )(q, k, v)


def _merge_kernel(xn_ref, oa_ref, ob_ref, oc_ref, od_ref, wga_ref, wgb_ref, wgc_ref, wgd_ref, bg_ref, wb_ref,
                  out_ref):
    xn = xn_ref[...]
    acc = None
    branches = ((oa_ref, wga_ref), (ob_ref, wgb_ref), (oc_ref, wgc_ref), (od_ref, wgd_ref))
    for j, (o_ref, wg_ref) in enumerate(branches):
        z = _dot_nt(xn, wg_ref[...]) + bg_ref[j:j + 1, :]
        g = 1.0 / (1.0 + jnp.exp(-z))
        u = jnp.dot(o_ref[...], wb_ref[j], preferred_element_type=jnp.float32)
        acc = g * u if acc is None else acc + g * u
    out_ref[...] = acc.astype(out_ref.dtype)


def _merge(xn, o_list, w_all, b_gate, w_branch, layer, tm=512, tc=512):
    t = xn.shape[0]
    nc = D_MODEL // tc
    gate_blk0 = (N_DIL * GROUP_COLS + REST_COLS) // tc
    o_spec = pl.BlockSpec((tm, BRANCH_WIDTH), lambda c, i: (i, 0))

    def wg_spec(j):
        return pl.BlockSpec((None, tc, D_MODEL), lambda c, i, j=j: (layer, gate_blk0 + j * nc + c, 0))

    return pl.pallas_call(
        _merge_kernel,
        grid=(nc, t // tm),
        in_specs=[pl.BlockSpec((tm, D_MODEL), lambda c, i: (i, 0)), o_spec, o_spec, o_spec, o_spec,
                  wg_spec(0), wg_spec(1), wg_spec(2), wg_spec(3),
                  pl.BlockSpec((None, N_BRANCH, tc), lambda c, i: (layer, 0, c)),
                  pl.BlockSpec((None, N_BRANCH, BRANCH_WIDTH, tc), lambda c, i: (layer, 0, 0, c))],
        out_specs=pl.BlockSpec((tm, tc), lambda c, i: (i, c)),
        out_shape=jax.ShapeDtypeStruct((t, D_MODEL), jnp.bfloat16),
        compiler_params=_cparams(2),
        name="gated_merge",
    )(xn, *o_list, w_all, w_all, w_all, w_all, b_gate, w_branch)


OUT_PROJ_SUBTILES = 2


def _out_proj_kernel(mixed_ref, w_ref, h_ref, g_ref, wr_hi_ref, wr_lo_ref, h_out, xp_out, probs_out):
    rows = mixed_ref.shape[0] // OUT_PROJ_SUBTILES
    half = D_MODEL // 2
    lane = lax.broadcasted_iota(jnp.int32, (1, LANES), 1)
    for t in range(OUT_PROJ_SUBTILES):
        sl = slice(t * rows, (t + 1) * rows)
        h = h_ref[sl, :] + jnp.dot(mixed_ref[sl, :], w_ref[...], preferred_element_type=jnp.float32)
        h_out[sl, :] = h
        xn = h * lax.rsqrt(jnp.mean(h * h, axis=-1, keepdims=True) + EPS) * g_ref[...]
        xp_out[sl, :] = pltpu.pack_elementwise([xn[:, :half], xn[:, half:]], packed_dtype=jnp.bfloat16)
        x_hi = xn.astype(jnp.bfloat16)
        x_lo = (xn - x_hi.astype(jnp.float32)).astype(jnp.bfloat16)
        logits = (jnp.dot(x_hi, wr_hi_ref[...], preferred_element_type=jnp.float32)
                  + jnp.dot(x_lo, wr_hi_ref[...], preferred_element_type=jnp.float32)
                  + jnp.dot(x_hi, wr_lo_ref[...], preferred_element_type=jnp.float32))
        logits = jnp.where(lane < N_EXPERTS, logits, NEG_INF)
        m = jnp.max(logits, axis=-1, keepdims=True)
        e = jnp.exp(logits - m)
        probs_out[sl, :] = e / jnp.sum(e, axis=-1, keepdims=True)


def _out_proj(mixed, w_out, h, g_ffn, wr_hi, wr_lo, layer, tm=512):
    t = h.shape[0]
    row = lambda w: pl.BlockSpec((tm, w), lambda i: (i, 0))
    lay = lambda a, b: pl.BlockSpec((None, a, b), lambda i: (layer, 0, 0))
    return pl.pallas_call(
        _out_proj_kernel,
        grid=(t // tm,),
        in_specs=[row(D_MODEL), lay(D_MODEL, D_MODEL), row(D_MODEL), lay(1, D_MODEL),
                  lay(D_MODEL, LANES), lay(D_MODEL, LANES)],
        out_specs=[row(D_MODEL), row(D_MODEL // 2), row(LANES)],
        out_shape=[jax.ShapeDtypeStruct((t, D_MODEL), jnp.float32),
                   jax.ShapeDtypeStruct((t, D_MODEL // 2), jnp.uint32),
                   jax.ShapeDtypeStruct((t, LANES), jnp.float32)],
        compiler_params=_cparams(1),
        name="out_proj_router",
    )(mixed, w_out, h, g_ffn, wr_hi, wr_lo)


def _routing_kernel(p_ref, idx_out, gate_out, tri_ref):
    rows, seq = p_ref.shape
    p = p_ref[...]
    tok = lax.broadcasted_iota(jnp.int32, (rows, seq), 1)

    def count(mask):
        return jnp.sum(jnp.where(mask, 1.0, 0.0), axis=-1, keepdims=True)

    def thr_body(i, thr):
        cand = thr | jnp.left_shift(jnp.int32(1), 30 - i)
        cand_f = lax.bitcast_convert_type(cand, jnp.float32)
        return jnp.where(count(p >= cand_f) >= CAPACITY, cand, thr)

    thr = lax.bitcast_convert_type(lax.fori_loop(0, 31, thr_body, jnp.zeros((rows, 1), jnp.int32)), jnp.float32)
    gt = p > thr
    eq = p == thr
    need = CAPACITY - count(gt)

    def tie_body(i, bound):
        cand = bound | jnp.left_shift(jnp.int32(1), 11 - i)
        return jnp.where(count(jnp.logical_and(eq, tok < cand)) < need, cand, bound)

    bound = lax.fori_loop(0, 12, tie_body, jnp.zeros((rows, 1), jnp.int32))
    sel = jnp.logical_or(gt, jnp.logical_and(eq, tok <= bound))

    r_i = lax.broadcasted_iota(jnp.int32, (seq, seq), 0)
    c_i = lax.broadcasted_iota(jnp.int32, (seq, seq), 1)
    tri_ref[...] = jnp.where(r_i < c_i, 1.0, 0.0).astype(jnp.bfloat16)
    self_bf = jnp.where(sel, 1.0, 0.0).astype(jnp.bfloat16)
    pos = jnp.dot(self_bf, tri_ref[...], preferred_element_type=jnp.float32)
    pos = jnp.where(sel, pos, -1.0)

    bf = jnp.bfloat16
    p1 = p.astype(bf).astype(jnp.float32)
    r1 = p - p1
    p2 = r1.astype(bf).astype(jnp.float32)
    p3 = (r1 - p2).astype(bf).astype(jnp.float32)
    t_hi = (tok[0:1] >> 6).astype(jnp.float32)
    t_lo = (tok[0:1] & 63).astype(jnp.float32)
    slot = lax.broadcasted_iota(jnp.int32, (CAPACITY, seq), 0).astype(jnp.float32)
    prow = lax.broadcasted_iota(jnp.int32, (8, seq), 0)
    for r in range(rows):
        onehot = jnp.where(pos[r:r + 1, :] == slot, 1.0, 0.0).astype(bf)
        payload = jnp.where(prow == 0, t_hi, jnp.where(prow == 1, t_lo, jnp.where(
            prow == 2, p1[r:r + 1], jnp.where(prow == 3, p2[r:r + 1], jnp.where(prow == 4, p3[r:r + 1], 0.0)))))
        g = lax.dot_general(payload.astype(bf), onehot, (((1,), (1,)), ((), ())),
                            preferred_element_type=jnp.float32)
        idx_out[r:r + 1, :] = (g[0:1] * 64.0 + g[1:2]).astype(jnp.int32)
        gate_out[r:r + 1, :] = g[2:3] + g[3:4] + g[4:5]


def _routing(probs_t):
    rows, seq = probs_t.shape
    rt = N_EXPERTS
    return pl.pallas_call(
        _routing_kernel,
        grid=(rows // rt,),
        in_specs=[pl.BlockSpec((rt, seq), lambda i: (i, 0))],
        out_specs=[pl.BlockSpec((rt, CAPACITY), lambda i: (i, 0)), pl.BlockSpec((rt, CAPACITY), lambda i: (i, 0))],
        out_shape=[jax.ShapeDtypeStruct((rows, CAPACITY), jnp.int32),
                   jax.ShapeDtypeStruct((rows, CAPACITY), jnp.float32)],
        scratch_shapes=[pltpu.VMEM((seq, seq), jnp.bfloat16)],
        compiler_params=_cparams(1),
        name="expert_choice_routing",
    )(probs_t)


def _gather_kernel(idx_ref, x_ref, o_ref):
    def body(i, c):
        t = idx_ref[0, i]
        o_ref[pl.ds(i, 1), :] = x_ref[pl.ds(t, 1), :]
        return c
    lax.fori_loop(0, CAPACITY, body, 0, unroll=8)


def _gather(idx, x_packed):
    b, e = idx.shape[:2]
    w = x_packed.shape[-1]
    return pl.pallas_call(
        _gather_kernel,
        grid=(b, e),
        in_specs=[pl.BlockSpec((None, None, 1, CAPACITY), lambda i, j: (i, j, 0, 0), memory_space=pltpu.SMEM),
                  pl.BlockSpec((None, SEQ, w), lambda i, j: (i, 0, 0))],
        out_specs=pl.BlockSpec((None, None, CAPACITY, w), lambda i, j: (i, j, 0, 0)),
        out_shape=jax.ShapeDtypeStruct((b, e, CAPACITY, w), jnp.uint32),
        compiler_params=_cparams(2),
        name="moe_gather",
    )(idx, x_packed)


def _ffn_kernel(x_ref, gate_ref, wg_ref, wu_ref, wd_ref, o_ref):
    bf = jnp.bfloat16
    rows = x_ref.shape[0] * x_ref.shape[1]
    xp = x_ref[...].reshape(rows, x_ref.shape[2])
    lo = pltpu.unpack_elementwise(xp, index=0, packed_dtype=bf, unpacked_dtype=jnp.float32).astype(bf)
    hi = pltpu.unpack_elementwise(xp, index=1, packed_dtype=bf, unpacked_dtype=jnp.float32).astype(bf)
    x = jnp.concatenate([lo, hi], axis=1)
    a = jnp.dot(x, wg_ref[...], preferred_element_type=jnp.float32)
    u = jnp.dot(x, wu_ref[...], preferred_element_type=jnp.float32)
    h = (a / (1.0 + jnp.exp(-a)) * u).astype(bf)
    y = jnp.dot(h, wd_ref[...], preferred_element_type=jnp.float32) * gate_ref[...].reshape(rows, 1)
    o_ref[...] = y.reshape(o_ref.shape).astype(o_ref.dtype)


def _ffn(xe, gate, w_gate, w_up, w_down, layer):
    b, e, c, w = xe.shape
    bb = 2 if b % 2 == 0 else 1
    return pl.pallas_call(
        _ffn_kernel,
        grid=(e, b // bb),
        in_specs=[pl.BlockSpec((bb, None, c, w), lambda j, i: (i, j, 0, 0)),
                  pl.BlockSpec((bb, None, c, 1), lambda j, i: (i, j, 0, 0)),
                  pl.BlockSpec((None, None, D_MODEL, EXPERT_FF), lambda j, i: (layer, j, 0, 0)),
                  pl.BlockSpec((None, None, D_MODEL, EXPERT_FF), lambda j, i: (layer, j, 0, 0)),
                  pl.BlockSpec((None, None, EXPERT_FF, D_MODEL), lambda j, i: (layer, j, 0, 0))],
        out_specs=pl.BlockSpec((bb, None, c, D_MODEL), lambda j, i: (i, j, 0, 0)),
        out_shape=jax.ShapeDtypeStruct((b, e, c, D_MODEL), jnp.bfloat16),
        compiler_params=_cparams(2),
        name="expert_ffn",
    )(xe, gate, w_gate, w_up, w_down)


def _combine_kernel(idx_ref, y_ref, h_ref, o_ref, sel_ref):
    n_slots = sel_ref.shape[1]
    chunk = 512

    @pl.when(pl.program_id(1) == 0)
    def _():
        for c in range(n_slots // chunk):
            tok = lax.broadcasted_iota(jnp.int32, (SEQ, chunk), 0)
            hit = tok == idx_ref[:, c * chunk:(c + 1) * chunk]
            sel_ref[:, c * chunk:(c + 1) * chunk] = jnp.where(hit, 1.0, 0.0).astype(sel_ref.dtype)

    o_ref[...] = h_ref[...] + jnp.dot(sel_ref[...], y_ref[...], preferred_element_type=jnp.float32)


def _combine(idx, ye, h, n_split=4):
    b, n_slots = idx.shape[0], idx.shape[-1]
    wd = D_MODEL // n_split
    return pl.pallas_call(
        _combine_kernel,
        grid=(b, n_split),
        in_specs=[pl.BlockSpec((None, 1, n_slots), lambda i, s: (i, 0, 0)),
                  pl.BlockSpec((None, n_slots, wd), lambda i, s: (i, 0, s)),
                  pl.BlockSpec((None, SEQ, wd), lambda i, s: (i, 0, s))],
        out_specs=pl.BlockSpec((None, SEQ, wd), lambda i, s: (i, 0, s)),
        out_shape=jax.ShapeDtypeStruct((b, SEQ, D_MODEL), jnp.float32),
        scratch_shapes=[pltpu.VMEM((SEQ, n_slots), jnp.bfloat16)],
        compiler_params=_cparams(2),
        name="moe_combine",
    )(idx, ye, h)


def _alibi_slopes(n):
    return jnp.exp2(-8.0 * jnp.arange(1, n + 1, dtype=jnp.float32) / n)


def _token_mixer(h, l, p, tabs):
    t = h.shape[0]
    b = t // SEQ
    bf = jnp.bfloat16
    xn = _rmsnorm(h, p["g_attn_norm"][l], bf)
    groups = _proj_dilated(xn, p["w_all"], l)
    proj = _proj_rest(xn, p["w_all"], l, tm=min(t, 2048))
    proj4 = proj.reshape(b, 1, SEQ, REST_COLS)

    slopes = _alibi_slopes(A_HEADS)
    outs, lses = [], []
    for g, (window, dil) in enumerate(A_DILATIONS):
        o, lse = _banded(groups[g], lambda j: (j, N_PAIRS + j, 2 * N_PAIRS + j), slopes * dil, slopes,
                         (window // 2) // dil, False, True, jnp.float32, "banded_dilated")
        outs.append(o)
        lses.append(lse)
    o_a = _a_combine(outs, lses)

    qb, kb, vb = _mla_prep(proj, p["g_mla_q"][l], p["g_mla_kv"][l], p["w_mla_uq"][l], p["w_mla_ukv"][l],
                           tabs["mla"], (MLA_NOPE + MLA_ROPE) ** -0.5)
    o_b = _dense_attention(qb, kb, vb,
                           pl.BlockSpec((SEQ, 2 * LANES), lambda i, j: (i, j)),
                           pl.BlockSpec((SEQ, 2 * LANES), lambda i, j: (i, j)),
                           pl.BlockSpec((SEQ, 2 * LANES), lambda i, j: (i, j)), True, "latent_attention")

    qc, kc, vc = _c_prep(proj, p["g_c_q"][l], p["g_c_k"][l], tabs["axial"], HEAD_DIM ** -0.5)
    o_c = _dense_attention(qc, kc, vc,
                           pl.BlockSpec((SEQ, LANES), lambda i, j: (i, j)),
                           pl.BlockSpec((SEQ, LANES), lambda i, j: (i, j // 2)),
                           pl.BlockSpec((SEQ, LANES), lambda i, j: (i, j // 2)), False, "axial_attention")

    (o_d,) = _banded(proj4, lambda j: (BLK_QD + j, BLK_KD + j // 2, BLK_VD + j // 2),
                     _alibi_slopes(D_Q_HEADS), p["sink_d"][l].astype(jnp.float32),
                     D_HALF_WINDOW, True, False, bf, "banded_sink")

    return _merge(xn, (o_a, o_b, o_c, o_d), p["w_all"], p["b_gate"], p["w_branch"], l)


def _layer(h, l, p, tabs):
    t = h.shape[0]
    b = t // SEQ
    mixed = _token_mixer(h, l, p, tabs)
    h, x_packed, probs = _out_proj(mixed, p["w_out"], h, p["g_ffn_norm"], p["wr_hi"], p["wr_lo"], l)
    probs_t = jnp.transpose(probs.reshape(b, SEQ, LANES)[:, :, :N_EXPERTS], (0, 2, 1)).reshape(b * N_EXPERTS, SEQ)
    idx, gate = _routing(probs_t)
    xe = _gather(idx.reshape(b, N_EXPERTS, 1, CAPACITY), x_packed.reshape(b, SEQ, D_MODEL // 2))
    ye = _ffn(xe, gate.reshape(b, N_EXPERTS, CAPACITY, 1), p["w_exp_gate"], p["w_exp_up"], p["w_exp_down"], l)
    out = _combine(idx.reshape(b, 1, N_EXPERTS * CAPACITY), ye.reshape(b, N_EXPERTS * CAPACITY, D_MODEL),
                   h.reshape(b, SEQ, D_MODEL))
    return out.reshape(t, D_MODEL)


def _prepare_params(w_in, w_mla_uq, w_mla_ukv, w_branch, w_out, w_router, w_exp_gate, w_exp_up, w_exp_down):
    bf = jnp.bfloat16
    p = {}
    p["w_all"] = _relayout_w_in(w_in)
    p["w_mla_uq"] = jnp.stack([_mla_uq_layout(w_mla_uq[l]) for l in range(DEPTH)]).astype(bf)
    p["w_mla_ukv"] = jnp.stack([_mla_ukv_layout(w_mla_ukv[l]) for l in range(DEPTH)]).astype(bf)
    p["w_branch"] = w_branch.astype(bf)
    p["w_out"] = w_out.astype(bf)
    wr = jnp.pad(w_router, ((0, 0), (0, 0), (0, LANES - N_EXPERTS)))
    p["wr_hi"] = wr.astype(bf)
    p["wr_lo"] = (wr - p["wr_hi"].astype(jnp.float32)).astype(bf)
    p["w_exp_gate"] = w_exp_gate.astype(bf)
    p["w_exp_up"] = w_exp_up.astype(bf)
    p["w_exp_down"] = w_exp_down.astype(bf)
    return p


def kernel(x, w_in, b_gate, g_attn_norm, g_ffn_norm, g_mla_q, g_mla_kv, w_mla_uq, w_mla_ukv, g_c_q, g_c_k, sink_d,
           w_branch, w_out, w_router, w_exp_gate, w_exp_up, w_exp_down, g_final):
    b, s, d = x.shape
    assert (s, d) == (SEQ, D_MODEL)
    p = _prepare_params(w_in, w_mla_uq, w_mla_ukv, w_branch, w_out, w_router, w_exp_gate, w_exp_up, w_exp_down)
    p.update(b_gate=b_gate, g_attn_norm=g_attn_norm, g_ffn_norm=g_ffn_norm.reshape(DEPTH, 1, D_MODEL),
             g_mla_q=g_mla_q, g_mla_kv=g_mla_kv, g_c_q=g_c_q, g_c_k=g_c_k, sink_d=sink_d)
    pos = jnp.arange(SEQ)
    rows, cols = pos // GRID_W, pos % GRID_W
    tabs = {
        "mla": _rope_tables([pos], MLA_ROPE, [MLA_NOPE]),
        "axial": _rope_tables([rows, cols, rows, cols], HEAD_DIM // 2, [0, 32, 64, 96]),
    }
    h = x.reshape(b * s, d)
    for l in range(DEPTH):
        h = _layer(h, l, p, tabs)
    return _rmsnorm(h, g_final, jnp.float32).reshape(b, s, d)
```

```python
import functools

import jax
import jax.numpy as jnp
from jax import lax
from jax.experimental import pallas as pl
from jax.experimental.pallas import tpu as pltpu

D_MODEL = 2048
SEQ = 2048
DEPTH = 2
HEAD_DIM = 64
GRID_W = 64
ROPE_THETA = 10000.0
NEG_INF = -1e30
EPS = 1e-6
N_BRANCH = 4
BRANCH_WIDTH = 512

A_HEADS = 8
A_DILATIONS = ((128, 1), (512, 4), (2048, 16))
N_DIL = 3
MLA_HEADS = 8
MLA_Q_LORA = 512
MLA_KV_LORA = 256
MLA_NOPE = 64
MLA_ROPE = 32
MLA_V = 64
C_Q_HEADS = 8
C_KV_HEADS = 2
D_Q_HEADS = 8
D_KV_HEADS = 2
D_HALF_WINDOW = 128

A_COLS = 3 * N_DIL * A_HEADS * HEAD_DIM
B_COLS = MLA_Q_LORA + MLA_KV_LORA + MLA_ROPE
C_COLS = (C_Q_HEADS + 2 * C_KV_HEADS) * HEAD_DIM
D_COLS = (D_Q_HEADS + 2 * D_KV_HEADS) * HEAD_DIM
GATE_COL0 = A_COLS + B_COLS + C_COLS + D_COLS

N_EXPERTS = 16
EXPERT_FF = D_MODEL // 2
CAPACITY = 2 * SEQ // N_EXPERTS

LANES = 128
N_PAIRS = 4
DEN_LANE = HEAD_DIM
VMEM_LIMIT = 56 * 1024 * 1024

GROUP_COLS = 3 * A_HEADS * HEAD_DIM
REST_COLS = 3072
GATE_COLS = N_BRANCH * D_MODEL
BLK_CQ = 0
BLK_QC = 4
BLK_QD = 8
BLK_CKV = 12
BLK_KPE = 14
BLK_KC = 16
BLK_VC = 18
BLK_KD = 20
BLK_VD = 22


def _cparams(n_grid_dims):
    return pltpu.CompilerParams(dimension_semantics=("arbitrary",) * n_grid_dims, vmem_limit_bytes=VMEM_LIMIT)


def _proj_segments():
    segs = []
    for g in range(N_DIL):
        for s in range(3):
            segs.append(((s * N_DIL + g) * A_HEADS * HEAD_DIM, A_HEADS * HEAD_DIM))
    b0 = A_COLS
    c0 = A_COLS + B_COLS
    d0 = c0 + C_COLS
    segs.append((b0, MLA_Q_LORA))
    segs.append((c0, C_Q_HEADS * HEAD_DIM))
    segs.append((d0, D_Q_HEADS * HEAD_DIM))
    segs.append((b0 + MLA_Q_LORA, MLA_KV_LORA))
    segs += [(None, 64), (b0 + MLA_Q_LORA + MLA_KV_LORA, MLA_ROPE), (None, 32 + LANES)]
    for base, nq, nkv in ((c0, C_Q_HEADS, C_KV_HEADS), (d0, D_Q_HEADS, D_KV_HEADS)):
        for part in range(2):
            for kv in range(nkv):
                start = base + (nq + part * nkv + kv) * HEAD_DIM
                segs += [(start, HEAD_DIM), (start, HEAD_DIM)]
    assert sum(w for _, w in segs) == N_DIL * GROUP_COLS + REST_COLS
    segs.append((GATE_COL0, GATE_COLS))
    return segs


def _relayout_w_in(w_in):
    wt = jnp.transpose(w_in, (0, 2, 1))
    parts = []
    for start, width in _proj_segments():
        if start is None:
            parts.append(jnp.zeros((wt.shape[0], width, wt.shape[2]), jnp.bfloat16))
        else:
            parts.append(wt[:, start:start + width, :].astype(jnp.bfloat16))
    return jnp.concatenate(parts, axis=1)


def _dot_nt(x, w):
    return lax.dot_general(x, w, (((1,), (1,)), ((), ())), preferred_element_type=jnp.float32)


def _mla_uq_layout(w_uq):
    w = w_uq.reshape(MLA_Q_LORA, MLA_HEADS, MLA_NOPE + MLA_ROPE)
    w = jnp.pad(w, ((0, 0), (0, 0), (0, LANES - MLA_NOPE - MLA_ROPE)))
    return w.reshape(MLA_Q_LORA, MLA_HEADS * LANES)


def _mla_ukv_layout(w_ukv):
    w = w_ukv.reshape(MLA_KV_LORA, MLA_HEADS, MLA_NOPE + MLA_V)
    wk = jnp.pad(w[:, :, :MLA_NOPE], ((0, 0), (0, 0), (0, LANES - MLA_NOPE)))
    wv = jnp.pad(w[:, :, MLA_NOPE:], ((0, 0), (0, 0), (0, LANES - MLA_V)))
    return jnp.concatenate([wk.reshape(MLA_KV_LORA, MLA_HEADS * LANES),
                            wv.reshape(MLA_KV_LORA, MLA_HEADS * LANES)], axis=1)


def _rope_tables(pos_list, dim, lane_offsets):
    half = dim // 2
    freqs = ROPE_THETA ** (-jnp.arange(0, dim, 2, dtype=jnp.float32) / dim)
    cos_t = jnp.ones((SEQ, LANES), jnp.float32)
    sin_a = jnp.zeros((SEQ, LANES), jnp.float32)
    sin_b = jnp.zeros((SEQ, LANES), jnp.float32)
    for pos, off in zip(pos_list, lane_offsets):
        ang = pos.astype(jnp.float32)[:, None] * freqs[None, :]
        c, s = jnp.cos(ang), jnp.sin(ang)
        cos_t = cos_t.at[:, off:off + half].set(c).at[:, off + half:off + dim].set(c)
        sin_a = sin_a.at[:, off:off + half].set(-s)
        sin_b = sin_b.at[:, off + half:off + dim].set(s)
    return cos_t, sin_a, sin_b


def _apply_rope(x, cos_t, sin_a, sin_b, half):
    n = x.shape[-1]
    return (x * cos_t + pltpu.roll(x, n - half, 1) * sin_a + pltpu.roll(x, half, 1) * sin_b)


def _rmsnorm_kernel(x_ref, g_ref, o_ref):
    x = x_ref[...]
    ms = jnp.mean(x * x, axis=-1, keepdims=True)
    o_ref[...] = (x * lax.rsqrt(ms + EPS) * g_ref[...]).astype(o_ref.dtype)


def _rmsnorm(x, g, out_dtype, tm=512):
    t, d = x.shape
    return pl.pallas_call(
        _rmsnorm_kernel,
        grid=(t // tm,),
        in_specs=[pl.BlockSpec((tm, d), lambda i: (i, 0)), pl.BlockSpec((1, d), lambda i: (0, 0))],
        out_specs=pl.BlockSpec((tm, d), lambda i: (i, 0)),
        out_shape=jax.ShapeDtypeStruct((t, d), out_dtype),
        compiler_params=_cparams(1),
        name="rmsnorm",
    )(x, g.reshape(1, d))


def _matmul_kernel(x_ref, w_ref, o_ref):
    o_ref[...] = _dot_nt(x_ref[...], w_ref[...]).astype(o_ref.dtype)


def _proj_rest(x, w_all, layer, tm, tn=1536):
    m, k = x.shape
    blk0 = N_DIL * GROUP_COLS // tn
    return pl.pallas_call(
        _matmul_kernel,
        grid=(m // tm, REST_COLS // tn),
        in_specs=[pl.BlockSpec((tm, k), lambda i, j: (i, 0)),
                  pl.BlockSpec((None, tn, k), lambda i, j: (layer, blk0 + j, 0))],
        out_specs=pl.BlockSpec((tm, tn), lambda i, j: (i, j)),
        out_shape=jax.ShapeDtypeStruct((m, REST_COLS), jnp.bfloat16),
        compiler_params=_cparams(2),
        name="in_proj",
    )(x, w_all)


def _proj_dilated_kernel(x_ref, w0_ref, w1_ref, w2_ref, o0_ref, o1_ref, o2_ref, scr_ref):
    x = x_ref[...]
    for w_ref, o_ref, (_, dil) in zip((w0_ref, w1_ref, w2_ref), (o0_ref, o1_ref, o2_ref), A_DILATIONS):
        acc = _dot_nt(x, w_ref[...])
        if dil == 1:
            o_ref[0] = acc.astype(o_ref.dtype)
        else:
            n_blk = acc.shape[1] // LANES
            for c in range(n_blk):
                scr_ref[c] = acc[:, c * LANES:(c + 1) * LANES]
            for r in range(dil):
                for c in range(n_blk):
                    rows = scr_ref[c, pl.ds(r, SEQ // dil, stride=dil), :]
                    o_ref[r, :, c * LANES:(c + 1) * LANES] = rows.astype(o_ref.dtype)


def _proj_dilated(x, w_all, layer, tn=512):
    t, k = x.shape
    b = t // SEQ
    nc = GROUP_COLS // tn

    def w_spec(g):
        return pl.BlockSpec((None, tn, k), lambda i, c, g=g: (layer, g * nc + c, 0))

    out_specs, out_shape = [], []
    for _, dil in A_DILATIONS:
        ld = SEQ // dil
        out_specs.append(pl.BlockSpec((None, dil, ld, tn), lambda i, c: (i, 0, 0, c)))
        out_shape.append(jax.ShapeDtypeStruct((b, dil, ld, GROUP_COLS), jnp.bfloat16))
    return pl.pallas_call(
        _proj_dilated_kernel,
        grid=(b, nc),
        in_specs=[pl.BlockSpec((SEQ, k), lambda i, c: (i, 0)), w_spec(0), w_spec(1), w_spec(2)],
        out_specs=out_specs,
        out_shape=out_shape,
        scratch_shapes=[pltpu.VMEM((tn // LANES, SEQ, LANES), jnp.float32)],
        compiler_params=_cparams(2),
        name="in_proj_dilated",
    )(x, w_all, w_all, w_all)


BANDED_GROUP_SCORE_ELEMS = 512 * 1024


def _banded_kernel(slope_ref, sink_ref, q_ref, k_ref, v_ref, *refs, length, half_window, dil, use_sink, want_lse):
    o_ref = refs[0]
    lse_ref = refs[1] if want_lse else None
    bias_ref = refs[-1]
    hp = pl.program_id(1)
    w = half_window
    tq = 2 * w
    tk = min(4 * w, length)
    nq = length // tq
    key_start = [min(max(qb * tq - w, 0), length - tk) for qb in range(nq)]
    offsets = sorted({key_start[qb] - qb * tq for qb in range(nq)})

    lane = lax.broadcasted_iota(jnp.int32, (1, LANES), 1)
    half0 = lane < HEAD_DIM
    delta = lax.broadcasted_iota(jnp.int32, (tq, tk), 1) - lax.broadcasted_iota(jnp.int32, (tq, tk), 0)
    slopes = [slope_ref[2 * hp + i] for i in range(2)]
    for oi, off in enumerate(offsets):
        rel = jnp.abs(delta + off)
        dist = rel.astype(jnp.float32)
        for i in range(2):
            bias_ref[2 * oi + i] = jnp.where(rel <= w, -slopes[i] * dist, NEG_INF)

    scale = jnp.asarray(HEAD_DIM ** -0.5, q_ref.dtype)
    blocks = [(r, qb) for r in range(dil) for qb in range(nq)]
    group = max(1, BANDED_GROUP_SCORE_ELEMS // (2 * tq * tk))
    for g0 in range(0, len(blocks), group):
        grp = blocks[g0:g0 + group]
        scores = []
        for r, qb in grp:
            qs, ks = qb * tq, key_start[qb]
            oi = offsets.index(ks - qs)
            q = q_ref[r, qs:qs + tq, :] * scale
            kw = k_ref[r, ks:ks + tk, :]
            for i in range(2):
                qi = jnp.where(half0 if i == 0 else jnp.logical_not(half0), q, jnp.zeros_like(q))
                scores.append(_dot_nt(qi, kw) + bias_ref[2 * oi + i])
        probs, maxes, dens = [], [], []
        for n, s in enumerate(scores):
            m = jnp.max(s, axis=-1, keepdims=True)
            if use_sink:
                sk = sink_ref[2 * hp + n % 2]
                m = jnp.maximum(m, sk)
            p = jnp.exp(s - m)
            den = jnp.sum(p, axis=-1, keepdims=True)
            if use_sink:
                den = den + jnp.exp(sk - m)
            probs.append(p.astype(v_ref.dtype))
            maxes.append(m)
            dens.append(den)
        for n, (r, qb) in enumerate(grp):
            qs, ks = qb * tq, key_start[qb]
            vw = v_ref[r, ks:ks + tk, :]
            outs = [jnp.dot(probs[2 * n + i], vw, preferred_element_type=jnp.float32) / dens[2 * n + i]
                    for i in range(2)]
            rows = pl.ds(qs, tq) if dil == 1 else pl.ds(r + qs * dil, tq, stride=dil)
            o_ref[rows, :] = jnp.where(half0, outs[0], outs[1]).astype(o_ref.dtype)
            if want_lse:
                lses = [maxes[2 * n + i] + jnp.log(dens[2 * n + i]) for i in range(2)]
                lse_ref[rows, :] = jnp.where(half0, lses[0], lses[1]).astype(lse_ref.dtype)


def _banded(qkv, blocks, slopes, sinks, half_window, use_sink, want_lse, out_dtype, name):
    b, dil, length, _ = qkv.shape
    tq = 2 * half_window
    tk = min(4 * half_window, length)
    n_off = 1 if tk == length else 3
    kern = functools.partial(_banded_kernel, length=length, half_window=half_window, dil=dil,
                             use_sink=use_sink, want_lse=want_lse)
    smem = pl.BlockSpec(memory_space=pltpu.SMEM)

    def spec(s):
        return pl.BlockSpec((None, dil, length, LANES), lambda i, j, s=s: (i, 0, 0, blocks(j)[s]))

    out_spec = pl.BlockSpec((SEQ, LANES), lambda i, j: (i, j))
    out_specs = [out_spec]
    out_shape = [jax.ShapeDtypeStruct((b * SEQ, N_PAIRS * LANES), out_dtype)]
    if want_lse:
        out_specs.append(out_spec)
        out_shape.append(jax.ShapeDtypeStruct((b * SEQ, N_PAIRS * LANES), jnp.float32))
    return pl.pallas_call(
        kern,
        grid=(b, N_PAIRS),
        in_specs=[smem, smem, spec(0), spec(1), spec(2)],
        out_specs=out_specs,
        out_shape=out_shape,
        scratch_shapes=[pltpu.VMEM((2 * n_off, tq, tk), jnp.float32)],
        compiler_params=_cparams(2),
        name=name,
    )(slopes, sinks, qkv, qkv, qkv)


def _mix_dilation_groups(outs, lses):
    la, lb, lc = lses
    m = jnp.maximum(jnp.maximum(la, lb), lc)
    ea, eb, ec = jnp.exp(la - m), jnp.exp(lb - m), jnp.exp(lc - m)
    num = ea * outs[0] + eb * outs[1] + ec * outs[2]
    return num / (ea + eb + ec)


def _mla_prep_kernel(cq_ref, ckv_ref, kpe_ref, gq_ref, gkv_ref, wuq_ref, wukv_ref, cos_ref, sa_ref, sb_ref,
                     q_out, k_out, v_out, *, scale):
    def norm(x, g):
        x = x.astype(jnp.float32)
        return (x * lax.rsqrt(jnp.mean(x * x, axis=-1, keepdims=True) + EPS) * g).astype(jnp.bfloat16)

    cos_t, sin_a, sin_b = cos_ref[...], sa_ref[...], sb_ref[...]
    half = MLA_ROPE // 2
    q = jnp.dot(norm(cq_ref[...], gq_ref[...]), wuq_ref[...], preferred_element_type=jnp.float32)
    kv = jnp.dot(norm(ckv_ref[...], gkv_ref[...]), wukv_ref[...], preferred_element_type=jnp.float32)
    kpe = _apply_rope(kpe_ref[...].astype(jnp.float32), cos_t, sin_a, sin_b, half)
    for h in range(MLA_HEADS):
        sl = slice(h * LANES, (h + 1) * LANES)
        q_out[:, sl] = (_apply_rope(q[:, sl], cos_t, sin_a, sin_b, half) * scale).astype(q_out.dtype)
        k_out[:, sl] = (kv[:, sl] + kpe).astype(k_out.dtype)
    lane = lax.broadcasted_iota(jnp.int32, (1, MLA_HEADS * LANES), 1)
    ones_lane = jnp.where(lane % LANES == DEN_LANE, 1.0, 0.0)
    v_out[...] = (kv[:, MLA_HEADS * LANES:] + ones_lane).astype(v_out.dtype)


def _mla_prep(proj, g_q, g_kv, w_uq, w_ukv, tables, scale, tm=512):
    t = proj.shape[0]
    nblk = SEQ // tm
    tab_spec = pl.BlockSpec((tm, LANES), lambda i: (i % nblk, 0))
    nq, nk, nv = MLA_HEADS * LANES, MLA_HEADS * LANES, MLA_HEADS * LANES
    return pl.pallas_call(
        functools.partial(_mla_prep_kernel, scale=scale),
        grid=(t // tm,),
        in_specs=[pl.BlockSpec((tm, MLA_Q_LORA), lambda i: (i, BLK_CQ * LANES // MLA_Q_LORA)),
                  pl.BlockSpec((tm, MLA_KV_LORA), lambda i: (i, BLK_CKV * LANES // MLA_KV_LORA)),
                  pl.BlockSpec((tm, LANES), lambda i: (i, BLK_KPE)),
                  pl.BlockSpec((1, MLA_Q_LORA), lambda i: (0, 0)),
                  pl.BlockSpec((1, MLA_KV_LORA), lambda i: (0, 0)),
                  pl.BlockSpec(w_uq.shape, lambda i: (0, 0)),
                  pl.BlockSpec(w_ukv.shape, lambda i: (0, 0)),
                  tab_spec, tab_spec, tab_spec],
        out_specs=[pl.BlockSpec((tm, nq), lambda i: (i, 0)),
                   pl.BlockSpec((tm, nk), lambda i: (i, 0)),
                   pl.BlockSpec((tm, nv), lambda i: (i, 0))],
        out_shape=[jax.ShapeDtypeStruct((t, nq), jnp.bfloat16),
                   jax.ShapeDtypeStruct((t, nk), jnp.bfloat16),
                   jax.ShapeDtypeStruct((t, nv), jnp.bfloat16)],
        compiler_params=_cparams(1),
        name="mla_prep",
    )(proj, proj, proj, g_q.reshape(1, -1), g_kv.reshape(1, -1), w_uq, w_ukv, *tables)


def _c_prep_kernel(q_ref, k_ref, v_ref, gq_ref, gk_ref, cos_ref, sa_ref, sb_ref, q_out, k_out, v_out, *, scale):
    lane = lax.broadcasted_iota(jnp.int32, (1, LANES), 1)
    half0 = lane < HEAD_DIM
    cos_t, sin_a, sin_b = cos_ref[...], sa_ref[...], sb_ref[...]
    ones_lane = jnp.where(lane == DEN_LANE, 1.0, 0.0).astype(v_out.dtype)
    for p in range(v_ref.shape[1] // LANES):
        sl = slice(p * LANES, (p + 1) * LANES)
        v_out[:, sl] = jnp.where(half0, v_ref[:, sl], ones_lane)

    def prep(x, g):
        x = x.astype(jnp.float32)
        sq = x * x
        s0 = jnp.sum(jnp.where(half0, sq, 0.0), axis=-1, keepdims=True)
        s1 = jnp.sum(jnp.where(half0, 0.0, sq), axis=-1, keepdims=True)
        inv = jnp.where(half0, lax.rsqrt(s0 / HEAD_DIM + EPS), lax.rsqrt(s1 / HEAD_DIM + EPS))
        return _apply_rope(x * inv * g, cos_t, sin_a, sin_b, HEAD_DIM // 4)

    for p in range(q_ref.shape[1] // LANES):
        sl = slice(p * LANES, (p + 1) * LANES)
        q_out[:, sl] = (prep(q_ref[:, sl], gq_ref[...]) * scale).astype(q_out.dtype)
    for p in range(k_ref.shape[1] // LANES):
        sl = slice(p * LANES, (p + 1) * LANES)
        k_out[:, sl] = prep(k_ref[:, sl], gk_ref[...]).astype(k_out.dtype)


def _c_prep(proj, g_q, g_k, tables, scale, tm=512):
    t = proj.shape[0]
    nblk = SEQ // tm
    tab_spec = pl.BlockSpec((tm, LANES), lambda i: (i % nblk, 0))
    qw, kw = N_PAIRS * LANES, C_KV_HEADS * LANES
    g_q2 = jnp.tile(g_q.reshape(1, HEAD_DIM), (1, 2))
    g_k2 = jnp.tile(g_k.reshape(1, HEAD_DIM), (1, 2))
    return pl.pallas_call(
        functools.partial(_c_prep_kernel, scale=scale),
        grid=(t // tm,),
        in_specs=[pl.BlockSpec((tm, qw), lambda i: (i, BLK_QC * LANES // qw)),
                  pl.BlockSpec((tm, kw), lambda i: (i, BLK_KC * LANES // kw)),
                  pl.BlockSpec((tm, kw), lambda i: (i, BLK_VC * LANES // kw)),
                  pl.BlockSpec((1, LANES), lambda i: (0, 0)),
                  pl.BlockSpec((1, LANES), lambda i: (0, 0)),
                  tab_spec, tab_spec, tab_spec],
        out_specs=[pl.BlockSpec((tm, qw), lambda i: (i, 0)), pl.BlockSpec((tm, kw), lambda i: (i, 0)),
                   pl.BlockSpec((tm, kw), lambda i: (i, 0))],
        out_shape=[jax.ShapeDtypeStruct((t, qw), jnp.bfloat16), jax.ShapeDtypeStruct((t, kw), jnp.bfloat16),
                   jax.ShapeDtypeStruct((t, kw), jnp.bfloat16)],
        compiler_params=_cparams(1),
        name="axial_prep",
    )(proj, proj, proj, g_q2, g_k2, *tables)


DENSE_TILES_PER_STEP = 2


def _dense_kernel(q_ref, k_ref, v_ref, o_ref, *, tq, head_blocks):
    lane = lax.broadcasted_iota(jnp.int32, (1, LANES), 1)
    half0 = lane < HEAD_DIM
    seq = q_ref.shape[0]

    def head_operand(ref, i):
        return ref[:, i * LANES:(i + 1) * LANES] if head_blocks else ref[...]

    def body(t, carry):
        starts = [pl.multiple_of((t * DENSE_TILES_PER_STEP + u) * tq, tq) for u in range(DENSE_TILES_PER_STEP)]
        scores = []
        for qs in starts:
            q = q_ref[pl.ds(qs, tq), :]
            for i in range(2):
                if head_blocks:
                    qi = q[:, i * LANES:(i + 1) * LANES]
                else:
                    qi = jnp.where(half0 if i == 0 else jnp.logical_not(half0), q, jnp.zeros_like(q))
                scores.append(_dot_nt(qi, head_operand(k_ref, i)))
        for u, qs in enumerate(starts):
            outs = []
            for i in range(2):
                s = scores[2 * u + i]
                vi = head_operand(v_ref, i)
                m = jnp.max(s, axis=-1, keepdims=True)
                p = jnp.exp((s - m).astype(vi.dtype))
                pv = jnp.dot(p, vi, preferred_element_type=jnp.float32)
                outs.append(pv * (1.0 / pv[:, DEN_LANE:DEN_LANE + 1]))
            o_ref[pl.ds(qs, tq), :] = jnp.where(half0, outs[0], pltpu.roll(outs[1], HEAD_DIM, 1)).astype(o_ref.dtype)
        return carry

    lax.fori_loop(0, seq // (tq * DENSE_TILES_PER_STEP), body, 0)


def _dense_attention(q, k, v, q_spec, k_spec, v_spec, head_blocks, name, tq=256):
    t = q.shape[0]
    b = t // SEQ
    return pl.pallas_call(
        functools.partial(_dense_kernel, tq=tq, head_blocks=head_blocks),
        grid=(b, N_PAIRS),
        in_specs=[q_spec, k_spec, v_spec],
        out_specs=pl.BlockSpec((SEQ, LANES), lambda i, j: (i, j)),
        out_shape=jax.ShapeDtypeStruct((t, N_PAIRS * LANES), jnp.bfloat16),
        compiler_params=_cparams(2),
        name=name,
    )(q, k, v)


def _merge_kernel(xn_ref, oa0_ref, oa1_ref, oa2_ref, la0_ref, la1_ref, la2_ref, ob_ref, oc_ref, od_ref,
                  wga_ref, wgb_ref, wgc_ref, wgd_ref, bg_ref, wb_ref, out_ref, oa_ref):
    @pl.when(pl.program_id(1) == 0)
    def _():
        oa_ref[...] = _mix_dilation_groups([r[...] for r in (oa0_ref, oa1_ref, oa2_ref)],
                                           [r[...] for r in (la0_ref, la1_ref, la2_ref)]).astype(oa_ref.dtype)

    xn = xn_ref[...]
    acc = None
    for j, (o_ref, wg_ref) in enumerate(((oa_ref, wga_ref), (ob_ref, wgb_ref), (oc_ref, wgc_ref), (od_ref, wgd_ref))):
        z = _dot_nt(xn, wg_ref[...]) + bg_ref[j:j + 1, :]
        g = 1.0 / (1.0 + jnp.exp(-z))
        u = jnp.dot(o_ref[...], wb_ref[j], preferred_element_type=jnp.float32)
        acc = g * u if acc is None else acc + g * u
    out_ref[...] = acc.astype(out_ref.dtype)


def _merge(xn, a_outs, a_lses, o_rest, w_all, b_gate, w_branch, layer, tm=512, tc=512):
    t = xn.shape[0]
    nc = D_MODEL // tc
    gate_blk0 = (N_DIL * GROUP_COLS + REST_COLS) // tc
    o_spec = pl.BlockSpec((tm, BRANCH_WIDTH), lambda i, c: (i, 0))

    def wg_spec(j):
        return pl.BlockSpec((None, tc, D_MODEL), lambda i, c, j=j: (layer, gate_blk0 + j * nc + c, 0))

    return pl.pallas_call(
        _merge_kernel,
        grid=(t // tm, nc),
        in_specs=[pl.BlockSpec((tm, D_MODEL), lambda i, c: (i, 0))] + [o_spec] * 9
        + [wg_spec(0), wg_spec(1), wg_spec(2), wg_spec(3),
           pl.BlockSpec((None, N_BRANCH, tc), lambda i, c: (layer, 0, c)),
           pl.BlockSpec((None, N_BRANCH, BRANCH_WIDTH, tc), lambda i, c: (layer, 0, 0, c))],
        out_specs=pl.BlockSpec((tm, tc), lambda i, c: (i, c)),
        out_shape=jax.ShapeDtypeStruct((t, D_MODEL), jnp.bfloat16),
        scratch_shapes=[pltpu.VMEM((tm, BRANCH_WIDTH), jnp.bfloat16)],
        compiler_params=_cparams(2),
        name="gated_merge",
    )(xn, *a_outs, *a_lses, *o_rest, w_all, w_all, w_all, w_all, b_gate, w_branch)


OUT_PROJ_SUBTILES = 2


def _out_proj_kernel(mixed_ref, w_ref, h_ref, g_ref, wr_hi_ref, wr_lo_ref, h_out, xp_out, probs_out):
    rows = mixed_ref.shape[0] // OUT_PROJ_SUBTILES
    half = D_MODEL // 2
    lane = lax.broadcasted_iota(jnp.int32, (1, LANES), 1)
    for t in range(OUT_PROJ_SUBTILES):
        sl = slice(t * rows, (t + 1) * rows)
        h = h_ref[sl, :] + jnp.dot(mixed_ref[sl, :], w_ref[...], preferred_element_type=jnp.float32)
        h_out[sl, :] = h
        xn = h * lax.rsqrt(jnp.mean(h * h, axis=-1, keepdims=True) + EPS) * g_ref[...]
        xp_out[sl, :] = pltpu.pack_elementwise([xn[:, :half], xn[:, half:]], packed_dtype=jnp.bfloat16)
        x_hi = xn.astype(jnp.bfloat16)
        x_lo = (xn - x_hi.astype(jnp.float32)).astype(jnp.bfloat16)
        logits = (jnp.dot(x_hi, wr_hi_ref[...], preferred_element_type=jnp.float32)
                  + jnp.dot(x_lo, wr_hi_ref[...], preferred_element_type=jnp.float32)
                  + jnp.dot(x_hi, wr_lo_ref[...], preferred_element_type=jnp.float32))
        logits = jnp.where(lane < N_EXPERTS, logits, NEG_INF)
        m = jnp.max(logits, axis=-1, keepdims=True)
        e = jnp.exp(logits - m)
        probs_out[sl, :] = e / jnp.sum(e, axis=-1, keepdims=True)


def _out_proj(mixed, w_out, h, g_ffn, wr_hi, wr_lo, layer, tm=512):
    t = h.shape[0]
    row = lambda w: pl.BlockSpec((tm, w), lambda i: (i, 0))
    lay = lambda a, b: pl.BlockSpec((None, a, b), lambda i: (layer, 0, 0))
    return pl.pallas_call(
        _out_proj_kernel,
        grid=(t // tm,),
        in_specs=[row(D_MODEL), lay(D_MODEL, D_MODEL), row(D_MODEL), lay(1, D_MODEL),
                  lay(D_MODEL, LANES), lay(D_MODEL, LANES)],
        out_specs=[row(D_MODEL), row(D_MODEL // 2), row(LANES)],
        out_shape=[jax.ShapeDtypeStruct((t, D_MODEL), jnp.float32),
                   jax.ShapeDtypeStruct((t, D_MODEL // 2), jnp.uint32),
                   jax.ShapeDtypeStruct((t, LANES), jnp.float32)],
        compiler_params=_cparams(1),
        name="out_proj_router",
    )(mixed, w_out, h, g_ffn, wr_hi, wr_lo)


def _routing_kernel(p_ref, idx_out, gate_out, tri_ref):
    rows, seq = p_ref.shape
    p = p_ref[...]
    tok = lax.broadcasted_iota(jnp.int32, (rows, seq), 1)

    def count(mask):
        return jnp.sum(jnp.where(mask, 1.0, 0.0), axis=-1, keepdims=True)

    def thr_body(i, thr):
        cand = thr | jnp.left_shift(jnp.int32(1), 30 - i)
        cand_f = lax.bitcast_convert_type(cand, jnp.float32)
        return jnp.where(count(p >= cand_f) >= CAPACITY, cand, thr)

    thr = lax.bitcast_convert_type(lax.fori_loop(0, 31, thr_body, jnp.zeros((rows, 1), jnp.int32)), jnp.float32)
    gt = p > thr
    eq = p == thr
    need = CAPACITY - count(gt)

    def tie_body(i, bound):
        cand = bound | jnp.left_shift(jnp.int32(1), 11 - i)
        return jnp.where(count(jnp.logical_and(eq, tok < cand)) < need, cand, bound)

    bound = lax.fori_loop(0, 12, tie_body, jnp.zeros((rows, 1), jnp.int32))
    sel = jnp.logical_or(gt, jnp.logical_and(eq, tok <= bound))

    r_i = lax.broadcasted_iota(jnp.int32, (seq, seq), 0)
    c_i = lax.broadcasted_iota(jnp.int32, (seq, seq), 1)
    tri_ref[...] = jnp.where(r_i < c_i, 1.0, 0.0).astype(jnp.bfloat16)
    self_bf = jnp.where(sel, 1.0, 0.0).astype(jnp.bfloat16)
    pos = jnp.dot(self_bf, tri_ref[...], preferred_element_type=jnp.float32)
    pos = jnp.where(sel, pos, -1.0)

    bf = jnp.bfloat16
    p1 = p.astype(bf).astype(jnp.float32)
    r1 = p - p1
    p2 = r1.astype(bf).astype(jnp.float32)
    p3 = (r1 - p2).astype(bf).astype(jnp.float32)
    t_hi = (tok[0:1] >> 6).astype(jnp.float32)
    t_lo = (tok[0:1] & 63).astype(jnp.float32)
    slot = lax.broadcasted_iota(jnp.int32, (CAPACITY, seq), 0).astype(jnp.float32)
    prow = lax.broadcasted_iota(jnp.int32, (8, seq), 0)
    for r in range(rows):
        onehot = jnp.where(pos[r:r + 1, :] == slot, 1.0, 0.0).astype(bf)
        payload = jnp.where(prow == 0, t_hi, jnp.where(prow == 1, t_lo, jnp.where(
            prow == 2, p1[r:r + 1], jnp.where(prow == 3, p2[r:r + 1], jnp.where(prow == 4, p3[r:r + 1], 0.0)))))
        g = lax.dot_general(payload.astype(bf), onehot, (((1,), (1,)), ((), ())),
                            preferred_element_type=jnp.float32)
        idx_out[r:r + 1, :] = (g[0:1] * 64.0 + g[1:2]).astype(jnp.int32)
        gate_out[r:r + 1, :] = g[2:3] + g[3:4] + g[4:5]


def _routing(probs_t):
    rows, seq = probs_t.shape
    rt = N_EXPERTS
    return pl.pallas_call(
        _routing_kernel,
        grid=(rows // rt,),
        in_specs=[pl.BlockSpec((rt, seq), lambda i: (i, 0))],
        out_specs=[pl.BlockSpec((rt, CAPACITY), lambda i: (i, 0)), pl.BlockSpec((rt, CAPACITY), lambda i: (i, 0))],
        out_shape=[jax.ShapeDtypeStruct((rows, CAPACITY), jnp.int32),
                   jax.ShapeDtypeStruct((rows, CAPACITY), jnp.float32)],
        scratch_shapes=[pltpu.VMEM((seq, seq), jnp.bfloat16)],
        compiler_params=_cparams(1),
        name="expert_choice_routing",
    )(probs_t)


def _gather_kernel(idx_ref, x_ref, o_ref):
    def body(i, c):
        t = idx_ref[0, i]
        o_ref[pl.ds(i, 1), :] = x_ref[pl.ds(t, 1), :]
        return c
    lax.fori_loop(0, CAPACITY, body, 0, unroll=8)


def _gather(idx, x_packed):
    b, e = idx.shape[:2]
    w = x_packed.shape[-1]
    return pl.pallas_call(
        _gather_kernel,
        grid=(b, e),
        in_specs=[pl.BlockSpec((None, None, 1, CAPACITY), lambda i, j: (i, j, 0, 0), memory_space=pltpu.SMEM),
                  pl.BlockSpec((None, SEQ, w), lambda i, j: (i, 0, 0))],
        out_specs=pl.BlockSpec((None, None, CAPACITY, w), lambda i, j: (i, j, 0, 0)),
        out_shape=jax.ShapeDtypeStruct((b, e, CAPACITY, w), jnp.uint32),
        compiler_params=_cparams(2),
        name="moe_gather",
    )(idx, x_packed)


FFN_CHUNKS = 4


def _ffn_kernel(x_ref, gate_ref, wg_ref, wu_ref, wd_ref, o_ref, xs_ref, hs_ref):
    bf = jnp.bfloat16
    f = pl.program_id(2)
    rows = xs_ref.shape[0]

    @pl.when(f == 0)
    def _():
        xp = x_ref[...].reshape(rows, x_ref.shape[2])
        lo = pltpu.unpack_elementwise(xp, index=0, packed_dtype=bf, unpacked_dtype=jnp.float32).astype(bf)
        hi = pltpu.unpack_elementwise(xp, index=1, packed_dtype=bf, unpacked_dtype=jnp.float32).astype(bf)
        xs_ref[...] = jnp.concatenate([lo, hi], axis=1)

    x = xs_ref[...]
    a = jnp.dot(x, wg_ref[...].astype(bf), preferred_element_type=jnp.float32)
    u = jnp.dot(x, wu_ref[...].astype(bf), preferred_element_type=jnp.float32)
    hs_ref[f] = (a / (1.0 + jnp.exp(-a)) * u).astype(bf)

    @pl.when(f == FFN_CHUNKS - 1)
    def _():
        h_all = jnp.concatenate([hs_ref[c] for c in range(FFN_CHUNKS)], axis=1)
        y = jnp.dot(h_all, wd_ref[...].astype(bf), preferred_element_type=jnp.float32)
        out = y * gate_ref[...].reshape(rows, 1)
        o_ref[...] = out.reshape(o_ref.shape).astype(o_ref.dtype)


def _ffn(xe, gate, w_gate, w_up, w_down, layer):
    b, e, c, w = xe.shape
    bb = 4 if b % 4 == 0 else 1
    fc = EXPERT_FF // FFN_CHUNKS
    return pl.pallas_call(
        _ffn_kernel,
        grid=(e, b // bb, FFN_CHUNKS),
        in_specs=[pl.BlockSpec((bb, None, c, w), lambda j, i, f: (i, j, 0, 0)),
                  pl.BlockSpec((bb, None, c, 1), lambda j, i, f: (i, j, 0, 0)),
                  pl.BlockSpec((None, None, D_MODEL, fc), lambda j, i, f: (layer, j, 0, f)),
                  pl.BlockSpec((None, None, D_MODEL, fc), lambda j, i, f: (layer, j, 0, f)),
                  pl.BlockSpec((None, None, EXPERT_FF, D_MODEL), lambda j, i, f: (layer, j, 0, 0))],
        out_specs=pl.BlockSpec((bb, None, c, D_MODEL), lambda j, i, f: (i, j, 0, 0)),
        out_shape=jax.ShapeDtypeStruct((b, e, c, D_MODEL), jnp.bfloat16),
        scratch_shapes=[pltpu.VMEM((bb * c, D_MODEL), jnp.bfloat16),
                        pltpu.VMEM((FFN_CHUNKS, bb * c, fc), jnp.bfloat16)],
        compiler_params=_cparams(3),
        name="expert_ffn",
    )(xe, gate, w_gate, w_up, w_down)


def _combine_kernel(idx_ref, y_ref, h_ref, g_ref, *out_refs, final):
    tm = h_ref.shape[0]
    tok = pl.program_id(1) * tm + lax.broadcasted_iota(jnp.int32, (tm, idx_ref.shape[1]), 0)
    sel = jnp.where(tok == idx_ref[...], 1.0, 0.0).astype(y_ref.dtype)
    h = h_ref[...] + jnp.dot(sel, y_ref[...], preferred_element_type=jnp.float32)
    xn = h * lax.rsqrt(jnp.mean(h * h, axis=-1, keepdims=True) + EPS) * g_ref[...]
    if final:
        out_refs[0][...] = xn.astype(out_refs[0].dtype)
    else:
        out_refs[0][...] = h
        out_refs[1][...] = xn.astype(out_refs[1].dtype)


def _combine(idx, ye, h, g_next, final, tm=256):
    b, n_slots = idx.shape[0], idx.shape[-1]
    row = pl.BlockSpec((None, tm, D_MODEL), lambda i, r: (i, r, 0))
    shape = lambda dt: jax.ShapeDtypeStruct((b, SEQ, D_MODEL), dt)
    return pl.pallas_call(
        functools.partial(_combine_kernel, final=final),
        grid=(b, SEQ // tm),
        in_specs=[pl.BlockSpec((None, 1, n_slots), lambda i, r: (i, 0, 0)),
                  pl.BlockSpec((None, n_slots, D_MODEL), lambda i, r: (i, 0, 0)),
                  row,
                  pl.BlockSpec((1, D_MODEL), lambda i, r: (0, 0))],
        out_specs=[row] if final else [row, row],
        out_shape=[shape(jnp.float32)] if final else [shape(jnp.float32), shape(jnp.bfloat16)],
        compiler_params=_cparams(2),
        name="moe_combine",
    )(idx, ye, h, g_next.reshape(1, D_MODEL))


def _alibi_slopes(n):
    return jnp.exp2(-8.0 * jnp.arange(1, n + 1, dtype=jnp.float32) / n)


def _token_mixer(xn, l, p, tabs):
    t = xn.shape[0]
    b = t // SEQ
    bf = jnp.bfloat16
    groups = _proj_dilated(xn, p["w_all"], l)
    proj = _proj_rest(xn, p["w_all"], l, tm=min(t, 2048))
    proj4 = proj.reshape(b, 1, SEQ, REST_COLS)

    slopes = _alibi_slopes(A_HEADS)
    outs, lses = [], []
    for g, (window, dil) in enumerate(A_DILATIONS):
        o, lse = _banded(groups[g], lambda j: (j, N_PAIRS + j, 2 * N_PAIRS + j), slopes * dil, slopes,
                         (window // 2) // dil, False, True, jnp.float32, "banded_dilated")
        outs.append(o)
        lses.append(lse)

    qb, kb, vb = _mla_prep(proj, p["g_mla_q"][l], p["g_mla_kv"][l], p["w_mla_uq"][l], p["w_mla_ukv"][l],
                           tabs["mla"], (MLA_NOPE + MLA_ROPE) ** -0.5)
    o_b = _dense_attention(qb, kb, vb,
                           pl.BlockSpec((SEQ, 2 * LANES), lambda i, j: (i, j)),
                           pl.BlockSpec((SEQ, 2 * LANES), lambda i, j: (i, j)),
                           pl.BlockSpec((SEQ, 2 * LANES), lambda i, j: (i, j)), True, "latent_attention")

    qc, kc, vc = _c_prep(proj, p["g_c_q"][l], p["g_c_k"][l], tabs["axial"], HEAD_DIM ** -0.5)
    o_c = _dense_attention(qc, kc, vc,
                           pl.BlockSpec((SEQ, LANES), lambda i, j: (i, j)),
                           pl.BlockSpec((SEQ, LANES), lambda i, j: (i, j // 2)),
                           pl.BlockSpec((SEQ, LANES), lambda i, j: (i, j // 2)), False, "axial_attention")

    (o_d,) = _banded(proj4, lambda j: (BLK_QD + j, BLK_KD + j // 2, BLK_VD + j // 2),
                     _alibi_slopes(D_Q_HEADS), p["sink_d"][l].astype(jnp.float32),
                     D_HALF_WINDOW, True, False, bf, "banded_sink")

    return _merge(xn, outs, lses, (o_b, o_c, o_d), p["w_all"], p["b_gate"], p["w_branch"], l)


def _layer(h, xn, l, p, tabs, g_next, final):
    t = h.shape[0]
    b = t // SEQ
    mixed = _token_mixer(xn, l, p, tabs)
    h, x_packed, probs = _out_proj(mixed, p["w_out"], h, p["g_ffn_norm"], p["wr_hi"], p["wr_lo"], l)
    probs_t = jnp.transpose(probs.reshape(b, SEQ, LANES)[:, :, :N_EXPERTS], (0, 2, 1)).reshape(b * N_EXPERTS, SEQ)
    idx, gate = _routing(probs_t)
    xe = _gather(idx.reshape(b, N_EXPERTS, 1, CAPACITY), x_packed.reshape(b, SEQ, D_MODEL // 2))
    ye = _ffn(xe, gate.reshape(b, N_EXPERTS, CAPACITY, 1), p["w_exp_gate"], p["w_exp_up"], p["w_exp_down"], l)
    outs = _combine(idx.reshape(b, 1, N_EXPERTS * CAPACITY), ye.reshape(b, N_EXPERTS * CAPACITY, D_MODEL),
                    h.reshape(b, SEQ, D_MODEL), g_next, final)
    return [o.reshape(t, D_MODEL) for o in outs]


def _prepare_params(w_in, w_mla_uq, w_mla_ukv, w_branch, w_out, w_router, w_exp_gate, w_exp_up, w_exp_down):
    bf = jnp.bfloat16
    p = {}
    p["w_all"] = _relayout_w_in(w_in)
    p["w_mla_uq"] = jnp.stack([_mla_uq_layout(w_mla_uq[l]) for l in range(DEPTH)]).astype(bf)
    p["w_mla_ukv"] = jnp.stack([_mla_ukv_layout(w_mla_ukv[l]) for l in range(DEPTH)]).astype(bf)
    p["w_branch"] = w_branch.astype(bf)
    p["w_out"] = w_out.astype(bf)
    wr = jnp.pad(w_router, ((0, 0), (0, 0), (0, LANES - N_EXPERTS)))
    p["wr_hi"] = wr.astype(bf)
    p["wr_lo"] = (wr - p["wr_hi"].astype(jnp.float32)).astype(bf)
    p["w_exp_gate"] = w_exp_gate
    p["w_exp_up"] = w_exp_up
    p["w_exp_down"] = w_exp_down
    return p


def kernel(x, w_in, b_gate, g_attn_norm, g_ffn_norm, g_mla_q, g_mla_kv, w_mla_uq, w_mla_ukv, g_c_q, g_c_k, sink_d,
           w_branch, w_out, w_router, w_exp_gate, w_exp_up, w_exp_down, g_final):
    b, s, d = x.shape
    assert (s, d) == (SEQ, D_MODEL)
    p = _prepare_params(w_in, w_mla_uq, w_mla_ukv, w_branch, w_out, w_router, w_exp_gate, w_exp_up, w_exp_down)
    p.update(b_gate=b_gate, g_attn_norm=g_attn_norm, g_ffn_norm=g_ffn_norm.reshape(DEPTH, 1, D_MODEL),
             g_mla_q=g_mla_q, g_mla_kv=g_mla_kv, g_c_q=g_c_q, g_c_k=g_c_k, sink_d=sink_d)
    pos = jnp.arange(SEQ)
    rows, cols = pos // GRID_W, pos % GRID_W
    tabs = {
        "mla": _rope_tables([pos], MLA_ROPE, [MLA_NOPE]),
        "axial": _rope_tables([rows, cols, rows, cols], HEAD_DIM // 2, [0, 32, 64, 96]),
    }
    h = x.reshape(b * s, d)
    xn = _rmsnorm(h, g_attn_norm[0], jnp.bfloat16)
    for l in range(DEPTH - 1):
        h, xn = _layer(h, xn, l, p, tabs, g_attn_norm[l + 1], False)
    (out,) = _layer(h, xn, DEPTH - 1, p, tabs, g_final, True)
    return out.reshape(b, s, d)
```

```python
import functools

import numpy as np
import jax
import jax.numpy as jnp
from jax import lax
from jax.experimental import pallas as pl
from jax.experimental.pallas import tpu as pltpu

D_MODEL = 2048
SEQ = 2048
DEPTH = 2
HEAD_DIM = 64
GRID_W = 64
ROPE_THETA = 10000.0
NEG_INF = -1e30
EPS = 1e-6
N_BRANCH = 4
BRANCH_WIDTH = 512

A_HEADS = 8
A_DILATIONS = ((128, 1), (512, 4), (2048, 16))
N_DIL = 3
MLA_HEADS = 8
MLA_Q_LORA = 512
MLA_KV_LORA = 256
MLA_NOPE = 64
MLA_ROPE = 32
MLA_V = 64
C_Q_HEADS = 8
C_KV_HEADS = 2
D_Q_HEADS = 8
D_KV_HEADS = 2
D_HALF_WINDOW = 128

A_COLS = 3 * N_DIL * A_HEADS * HEAD_DIM
B_COLS = MLA_Q_LORA + MLA_KV_LORA + MLA_ROPE
C_COLS = (C_Q_HEADS + 2 * C_KV_HEADS) * HEAD_DIM
D_COLS = (D_Q_HEADS + 2 * D_KV_HEADS) * HEAD_DIM
GATE_COL0 = A_COLS + B_COLS + C_COLS + D_COLS

N_EXPERTS = 16
EXPERT_FF = D_MODEL // 2
CAPACITY = 2 * SEQ // N_EXPERTS

LANES = 128
N_PAIRS = 4
DEN_LANE = HEAD_DIM
VMEM_LIMIT = 56 * 1024 * 1024

GROUP_COLS = 3 * A_HEADS * HEAD_DIM
REST_COLS = 3072
GATE_COLS = N_BRANCH * D_MODEL
BLK_CQ = 0
BLK_QC = 4
BLK_QD = 8
BLK_CKV = 12
BLK_KPE = 14
BLK_KC = 16
BLK_VC = 18
BLK_KD = 20
BLK_VD = 22


def _cparams(n_grid_dims):
    return pltpu.CompilerParams(dimension_semantics=("arbitrary",) * n_grid_dims, vmem_limit_bytes=VMEM_LIMIT)


def _proj_segments():
    segs = []
    for g in range(N_DIL):
        for s in range(3):
            segs.append(((s * N_DIL + g) * A_HEADS * HEAD_DIM, A_HEADS * HEAD_DIM))
    b0 = A_COLS
    c0 = A_COLS + B_COLS
    d0 = c0 + C_COLS
    segs.append((b0, MLA_Q_LORA))
    segs.append((c0, C_Q_HEADS * HEAD_DIM))
    segs.append((d0, D_Q_HEADS * HEAD_DIM))
    segs.append((b0 + MLA_Q_LORA, MLA_KV_LORA))
    segs += [(None, 64), (b0 + MLA_Q_LORA + MLA_KV_LORA, MLA_ROPE), (None, 32 + LANES)]
    for base, nq, nkv in ((c0, C_Q_HEADS, C_KV_HEADS), (d0, D_Q_HEADS, D_KV_HEADS)):
        for part in range(2):
            for kv in range(nkv):
                start = base + (nq + part * nkv + kv) * HEAD_DIM
                segs += [(start, HEAD_DIM), (start, HEAD_DIM)]
    assert sum(w for _, w in segs) == N_DIL * GROUP_COLS + REST_COLS
    segs.append((GATE_COL0, GATE_COLS))
    return segs


def _relayout_w_in(w_in):
    wt = jnp.transpose(w_in, (0, 2, 1))
    parts = []
    for start, width in _proj_segments():
        if start is None:
            parts.append(jnp.zeros((wt.shape[0], width, wt.shape[2]), jnp.bfloat16))
        else:
            parts.append(wt[:, start:start + width, :].astype(jnp.bfloat16))
    return jnp.concatenate(parts, axis=1)


def _dot_nt(x, w):
    return lax.dot_general(x, w, (((1,), (1,)), ((), ())), preferred_element_type=jnp.float32)


def _mla_uq_layout(w_uq):
    w = w_uq.reshape(MLA_Q_LORA, MLA_HEADS, MLA_NOPE + MLA_ROPE)
    w = jnp.pad(w, ((0, 0), (0, 0), (0, LANES - MLA_NOPE - MLA_ROPE)))
    return w.reshape(MLA_Q_LORA, MLA_HEADS * LANES)


def _mla_ukv_layout(w_ukv):
    w = w_ukv.reshape(MLA_KV_LORA, MLA_HEADS, MLA_NOPE + MLA_V)
    wk = jnp.pad(w[:, :, :MLA_NOPE], ((0, 0), (0, 0), (0, LANES - MLA_NOPE)))
    wv = jnp.pad(w[:, :, MLA_NOPE:], ((0, 0), (0, 0), (0, LANES - MLA_V)))
    return jnp.concatenate([wk.reshape(MLA_KV_LORA, MLA_HEADS * LANES),
                            wv.reshape(MLA_KV_LORA, MLA_HEADS * LANES)], axis=1)


def _rope_tables(pos_list, dim, lane_offsets):
    half = dim // 2
    freqs = ROPE_THETA ** (-np.arange(0, dim, 2, dtype=np.float64) / dim)
    cos_t = np.ones((SEQ, LANES), np.float64)
    sin_a = np.zeros((SEQ, LANES), np.float64)
    sin_b = np.zeros((SEQ, LANES), np.float64)
    for pos, off in zip(pos_list, lane_offsets):
        ang = np.asarray(pos, np.float64)[:, None] * freqs[None, :]
        c, s = np.cos(ang), np.sin(ang)
        cos_t[:, off:off + half] = c
        cos_t[:, off + half:off + dim] = c
        sin_a[:, off:off + half] = -s
        sin_b[:, off + half:off + dim] = s
    return tuple(jnp.asarray(t, jnp.float32) for t in (cos_t, sin_a, sin_b))


def _apply_rope(x, cos_t, sin_a, sin_b, half):
    n = x.shape[-1]
    return (x * cos_t + pltpu.roll(x, n - half, 1) * sin_a + pltpu.roll(x, half, 1) * sin_b)


def _rmsnorm_kernel(x_ref, g_ref, o_ref):
    x = x_ref[...]
    ms = jnp.mean(x * x, axis=-1, keepdims=True)
    o_ref[...] = (x * lax.rsqrt(ms + EPS) * g_ref[...]).astype(o_ref.dtype)


def _rmsnorm(x, g, out_dtype, tm=512):
    t, d = x.shape
    return pl.pallas_call(
        _rmsnorm_kernel,
        grid=(t // tm,),
        in_specs=[pl.BlockSpec((tm, d), lambda i: (i, 0)), pl.BlockSpec((1, d), lambda i: (0, 0))],
        out_specs=pl.BlockSpec((tm, d), lambda i: (i, 0)),
        out_shape=jax.ShapeDtypeStruct((t, d), out_dtype),
        compiler_params=_cparams(1),
        name="rmsnorm",
    )(x, g.reshape(1, d))


def _matmul_kernel(x_ref, w_ref, o_ref):
    o_ref[...] = _dot_nt(x_ref[...], w_ref[...]).astype(o_ref.dtype)


def _proj_rest(x, w_all, layer, tm, tn=1536):
    m, k = x.shape
    blk0 = N_DIL * GROUP_COLS // tn
    return pl.pallas_call(
        _matmul_kernel,
        grid=(m // tm, REST_COLS // tn),
        in_specs=[pl.BlockSpec((tm, k), lambda i, j: (i, 0)),
                  pl.BlockSpec((None, tn, k), lambda i, j: (layer, blk0 + j, 0))],
        out_specs=pl.BlockSpec((tm, tn), lambda i, j: (i, j)),
        out_shape=jax.ShapeDtypeStruct((m, REST_COLS), jnp.bfloat16),
        compiler_params=_cparams(2),
        name="in_proj",
    )(x, w_all)


def _proj_dilated_kernel(x_ref, w0_ref, w1_ref, w2_ref, o0_ref, o1_ref, o2_ref, scr_ref):
    x = x_ref[...]
    for w_ref, o_ref, (_, dil) in zip((w0_ref, w1_ref, w2_ref), (o0_ref, o1_ref, o2_ref), A_DILATIONS):
        acc = _dot_nt(x, w_ref[...])
        if dil == 1:
            o_ref[0] = acc.astype(o_ref.dtype)
        else:
            n_blk = acc.shape[1] // LANES
            for c in range(n_blk):
                scr_ref[c] = acc[:, c * LANES:(c + 1) * LANES]
            for r in range(dil):
                for c in range(n_blk):
                    rows = scr_ref[c, pl.ds(r, SEQ // dil, stride=dil), :]
                    o_ref[r, :, c * LANES:(c + 1) * LANES] = rows.astype(o_ref.dtype)


def _proj_dilated(x, w_all, layer, tn=512):
    t, k = x.shape
    b = t // SEQ
    nc = GROUP_COLS // tn

    def w_spec(g):
        return pl.BlockSpec((None, tn, k), lambda i, c, g=g: (layer, g * nc + c, 0))

    out_specs, out_shape = [], []
    for _, dil in A_DILATIONS:
        ld = SEQ // dil
        out_specs.append(pl.BlockSpec((None, dil, ld, tn), lambda i, c: (i, 0, 0, c)))
        out_shape.append(jax.ShapeDtypeStruct((b, dil, ld, GROUP_COLS), jnp.bfloat16))
    return pl.pallas_call(
        _proj_dilated_kernel,
        grid=(b, nc),
        in_specs=[pl.BlockSpec((SEQ, k), lambda i, c: (i, 0)), w_spec(0), w_spec(1), w_spec(2)],
        out_specs=out_specs,
        out_shape=out_shape,
        scratch_shapes=[pltpu.VMEM((tn // LANES, SEQ, LANES), jnp.float32)],
        compiler_params=_cparams(2),
        name="in_proj_dilated",
    )(x, w_all, w_all, w_all)


BANDED_GROUP_SCORE_ELEMS = 512 * 1024


def _banded_tiles(half_window, length):
    return 2 * half_window, min(4 * half_window, length)


def _banded_sequences(slopes, sinks, q_ref, k_ref, v_ref, bias_ref, write, *, length, half_window, dil):
    use_sink = sinks is not None
    w = half_window
    tq, tk = _banded_tiles(w, length)
    nq = length // tq
    key_start = [min(max(qb * tq - w, 0), length - tk) for qb in range(nq)]
    offsets = sorted({key_start[qb] - qb * tq for qb in range(nq)})

    lane = lax.broadcasted_iota(jnp.int32, (1, LANES), 1)
    half0 = lane < HEAD_DIM
    delta = lax.broadcasted_iota(jnp.int32, (tq, tk), 1) - lax.broadcasted_iota(jnp.int32, (tq, tk), 0)
    for oi, off in enumerate(offsets):
        rel = jnp.abs(delta + off)
        dist = rel.astype(jnp.float32)
        for i in range(2):
            bias_ref[2 * oi + i] = jnp.where(rel <= w, -slopes[i] * dist, NEG_INF)

    scale = jnp.asarray(HEAD_DIM ** -0.5, q_ref.dtype)
    blocks = [(r, qb) for r in range(dil) for qb in range(nq)]
    group = max(1, BANDED_GROUP_SCORE_ELEMS // (2 * tq * tk))
    for g0 in range(0, len(blocks), group):
        grp = blocks[g0:g0 + group]
        scores = []
        for r, qb in grp:
            qs, ks = qb * tq, key_start[qb]
            oi = offsets.index(ks - qs)
            q = q_ref[r, qs:qs + tq, :] * scale
            kw = k_ref[r, ks:ks + tk, :]
            for i in range(2):
                qi = jnp.where(half0 if i == 0 else jnp.logical_not(half0), q, jnp.zeros_like(q))
                scores.append(_dot_nt(qi, kw) + bias_ref[2 * oi + i])
        probs, maxes, dens = [], [], []
        for n, s in enumerate(scores):
            m = jnp.max(s, axis=-1, keepdims=True)
            if use_sink:
                sk = sinks[n % 2]
                m = jnp.maximum(m, sk)
            p = jnp.exp(s - m)
            den = jnp.sum(p, axis=-1, keepdims=True)
            if use_sink:
                den = den + jnp.exp(sk - m)
            probs.append(p.astype(v_ref.dtype))
            maxes.append(m)
            dens.append(den)
        for n, (r, qb) in enumerate(grp):
            qs, ks = qb * tq, key_start[qb]
            vw = v_ref[r, ks:ks + tk, :]
            outs = [jnp.dot(probs[2 * n + i], vw, preferred_element_type=jnp.float32) / dens[2 * n + i]
                    for i in range(2)]
            rows = pl.ds(qs, tq) if dil == 1 else pl.ds(r + qs * dil, tq, stride=dil)
            lses = [maxes[2 * n + i] + jnp.log(dens[2 * n + i]) for i in range(2)]
            write(rows, jnp.where(half0, outs[0], outs[1]), jnp.where(half0, lses[0], lses[1]))


def _bias_scratch(half_window, length):
    tq, tk = _banded_tiles(half_window, length)
    return pltpu.VMEM((2 if tk == length else 6, tq, tk), jnp.float32)


def _sink_kernel(slope_ref, sink_ref, q_ref, k_ref, v_ref, o_ref, bias_ref):
    hp = pl.program_id(1)

    def write(rows, o, lse):
        del lse
        o_ref[rows, :] = o.astype(o_ref.dtype)

    _banded_sequences([slope_ref[2 * hp + i] for i in range(2)], [sink_ref[2 * hp + i] for i in range(2)],
                      q_ref, k_ref, v_ref, bias_ref, write, length=SEQ, half_window=D_HALF_WINDOW, dil=1)


def _sink_attention(proj4, slopes, sinks):
    b = proj4.shape[0]
    smem = pl.BlockSpec(memory_space=pltpu.SMEM)

    def spec(blk_fn):
        return pl.BlockSpec((None, 1, SEQ, LANES), lambda i, j: (i, 0, 0, blk_fn(j)))

    return pl.pallas_call(
        _sink_kernel,
        grid=(b, N_PAIRS),
        in_specs=[smem, smem, spec(lambda j: BLK_QD + j), spec(lambda j: BLK_KD + j // 2),
                  spec(lambda j: BLK_VD + j // 2)],
        out_specs=pl.BlockSpec((SEQ, LANES), lambda i, j: (i, j)),
        out_shape=jax.ShapeDtypeStruct((b * SEQ, N_PAIRS * LANES), jnp.bfloat16),
        scratch_shapes=[_bias_scratch(D_HALF_WINDOW, SEQ)],
        compiler_params=_cparams(2),
        name="banded_sink",
    )(slopes, sinks, proj4, proj4, proj4)


def _dilated_kernel(slope_ref, *refs):
    qkv_refs = refs[:9]
    o_ref = refs[9]
    bias_refs = refs[10:13]
    o_stage, lse_stage = refs[13], refs[14]
    hp = pl.program_id(1)
    for g, (window, dil) in enumerate(A_DILATIONS):
        def write(rows, o, lse, g=g):
            o_stage[g, rows, :] = o
            lse_stage[g, rows, :] = lse

        _banded_sequences([slope_ref[g, 2 * hp + i] for i in range(2)], None, *qkv_refs[3 * g:3 * g + 3],
                          bias_refs[g], write, length=SEQ // dil, half_window=(window // 2) // dil, dil=dil)
    la, lb, lc = lse_stage[0], lse_stage[1], lse_stage[2]
    m = jnp.maximum(jnp.maximum(la, lb), lc)
    ea, eb, ec = jnp.exp(la - m), jnp.exp(lb - m), jnp.exp(lc - m)
    num = ea * o_stage[0] + eb * o_stage[1] + ec * o_stage[2]
    o_ref[...] = (num / (ea + eb + ec)).astype(o_ref.dtype)


def _dilated_attention(groups, slopes):
    b = groups[0].shape[0]
    in_specs = [pl.BlockSpec(memory_space=pltpu.SMEM)]
    operands = [slopes]
    scratch = []
    for g, (window, dil) in enumerate(A_DILATIONS):
        ld = SEQ // dil
        for s in range(3):
            in_specs.append(pl.BlockSpec((None, dil, ld, LANES), lambda i, j, s=s: (i, 0, 0, s * N_PAIRS + j)))
            operands.append(groups[g])
        scratch.append(_bias_scratch((window // 2) // dil, ld))
    scratch += [pltpu.VMEM((N_DIL, SEQ, LANES), jnp.float32)] * 2
    return pl.pallas_call(
        _dilated_kernel,
        grid=(b, N_PAIRS),
        in_specs=in_specs,
        out_specs=pl.BlockSpec((SEQ, LANES), lambda i, j: (i, j)),
        out_shape=jax.ShapeDtypeStruct((b * SEQ, N_PAIRS * LANES), jnp.bfloat16),
        scratch_shapes=scratch,
        compiler_params=_cparams(2),
        name="banded_dilated",
    )(*operands)


def _mla_prep_kernel(cq_ref, ckv_ref, kpe_ref, gq_ref, gkv_ref, wuq_ref, wukv_ref, cos_ref, sa_ref, sb_ref,
                     q_out, k_out, v_out, *, scale):
    def norm(x, g):
        x = x.astype(jnp.float32)
        return (x * lax.rsqrt(jnp.mean(x * x, axis=-1, keepdims=True) + EPS) * g).astype(jnp.bfloat16)

    cos_t, sin_a, sin_b = cos_ref[...], sa_ref[...], sb_ref[...]
    half = MLA_ROPE // 2
    q = jnp.dot(norm(cq_ref[...], gq_ref[...]), wuq_ref[...], preferred_element_type=jnp.float32)
    kv = jnp.dot(norm(ckv_ref[...], gkv_ref[...]), wukv_ref[...], preferred_element_type=jnp.float32)
    kpe = _apply_rope(kpe_ref[...].astype(jnp.float32), cos_t, sin_a, sin_b, half)
    for h in range(MLA_HEADS):
        sl = slice(h * LANES, (h + 1) * LANES)
        q_out[:, sl] = (_apply_rope(q[:, sl], cos_t, sin_a, sin_b, half) * scale).astype(q_out.dtype)
        k_out[:, sl] = (kv[:, sl] + kpe).astype(k_out.dtype)
    lane = lax.broadcasted_iota(jnp.int32, (1, MLA_HEADS * LANES), 1)
    ones_lane = jnp.where(lane % LANES == DEN_LANE, 1.0, 0.0)
    v_out[...] = (kv[:, MLA_HEADS * LANES:] + ones_lane).astype(v_out.dtype)


def _mla_prep(proj, g_q, g_kv, w_uq, w_ukv, tables, scale, tm=512):
    t = proj.shape[0]
    nblk = SEQ // tm
    tab_spec = pl.BlockSpec((tm, LANES), lambda i: (i % nblk, 0))
    nq, nk, nv = MLA_HEADS * LANES, MLA_HEADS * LANES, MLA_HEADS * LANES
    return pl.pallas_call(
        functools.partial(_mla_prep_kernel, scale=scale),
        grid=(t // tm,),
        in_specs=[pl.BlockSpec((tm, MLA_Q_LORA), lambda i: (i, BLK_CQ * LANES // MLA_Q_LORA)),
                  pl.BlockSpec((tm, MLA_KV_LORA), lambda i: (i, BLK_CKV * LANES // MLA_KV_LORA)),
                  pl.BlockSpec((tm, LANES), lambda i: (i, BLK_KPE)),
                  pl.BlockSpec((1, MLA_Q_LORA), lambda i: (0, 0)),
                  pl.BlockSpec((1, MLA_KV_LORA), lambda i: (0, 0)),
                  pl.BlockSpec(w_uq.shape, lambda i: (0, 0)),
                  pl.BlockSpec(w_ukv.shape, lambda i: (0, 0)),
                  tab_spec, tab_spec, tab_spec],
        out_specs=[pl.BlockSpec((tm, nq), lambda i: (i, 0)),
                   pl.BlockSpec((tm, nk), lambda i: (i, 0)),
                   pl.BlockSpec((tm, nv), lambda i: (i, 0))],
        out_shape=[jax.ShapeDtypeStruct((t, nq), jnp.bfloat16),
                   jax.ShapeDtypeStruct((t, nk), jnp.bfloat16),
                   jax.ShapeDtypeStruct((t, nv), jnp.bfloat16)],
        compiler_params=_cparams(1),
        name="mla_prep",
    )(proj, proj, proj, g_q.reshape(1, -1), g_kv.reshape(1, -1), w_uq, w_ukv, *tables)


def _c_prep_kernel(q_ref, k_ref, v_ref, gq_ref, gk_ref, cos_ref, sa_ref, sb_ref, q_out, k_out, v_out, *, scale):
    lane = lax.broadcasted_iota(jnp.int32, (1, LANES), 1)
    half0 = lane < HEAD_DIM
    cos_t, sin_a, sin_b = cos_ref[...], sa_ref[...], sb_ref[...]
    ones_lane = jnp.where(lane == DEN_LANE, 1.0, 0.0).astype(v_out.dtype)
    for p in range(v_ref.shape[1] // LANES):
        sl = slice(p * LANES, (p + 1) * LANES)
        v_out[:, sl] = jnp.where(half0, v_ref[:, sl], ones_lane)

    def prep(x, g):
        x = x.astype(jnp.float32)
        sq = x * x
        s0 = jnp.sum(jnp.where(half0, sq, 0.0), axis=-1, keepdims=True)
        s1 = jnp.sum(jnp.where(half0, 0.0, sq), axis=-1, keepdims=True)
        inv = jnp.where(half0, lax.rsqrt(s0 / HEAD_DIM + EPS), lax.rsqrt(s1 / HEAD_DIM + EPS))
        return _apply_rope(x * inv * g, cos_t, sin_a, sin_b, HEAD_DIM // 4)

    for p in range(q_ref.shape[1] // LANES):
        sl = slice(p * LANES, (p + 1) * LANES)
        q_out[:, sl] = (prep(q_ref[:, sl], gq_ref[...]) * scale).astype(q_out.dtype)
    for p in range(k_ref.shape[1] // LANES):
        sl = slice(p * LANES, (p + 1) * LANES)
        k_out[:, sl] = prep(k_ref[:, sl], gk_ref[...]).astype(k_out.dtype)


def _c_prep(proj, g_q, g_k, tables, scale, tm=512):
    t = proj.shape[0]
    nblk = SEQ // tm
    tab_spec = pl.BlockSpec((tm, LANES), lambda i: (i % nblk, 0))
    qw, kw = N_PAIRS * LANES, C_KV_HEADS * LANES
    g_q2 = jnp.tile(g_q.reshape(1, HEAD_DIM), (1, 2))
    g_k2 = jnp.tile(g_k.reshape(1, HEAD_DIM), (1, 2))
    return pl.pallas_call(
        functools.partial(_c_prep_kernel, scale=scale),
        grid=(t // tm,),
        in_specs=[pl.BlockSpec((tm, qw), lambda i: (i, BLK_QC * LANES // qw)),
                  pl.BlockSpec((tm, kw), lambda i: (i, BLK_KC * LANES // kw)),
                  pl.BlockSpec((tm, kw), lambda i: (i, BLK_VC * LANES // kw)),
                  pl.BlockSpec((1, LANES), lambda i: (0, 0)),
                  pl.BlockSpec((1, LANES), lambda i: (0, 0)),
                  tab_spec, tab_spec, tab_spec],
        out_specs=[pl.BlockSpec((tm, qw), lambda i: (i, 0)), pl.BlockSpec((tm, kw), lambda i: (i, 0)),
                   pl.BlockSpec((tm, kw), lambda i: (i, 0))],
        out_shape=[jax.ShapeDtypeStruct((t, qw), jnp.bfloat16), jax.ShapeDtypeStruct((t, kw), jnp.bfloat16),
                   jax.ShapeDtypeStruct((t, kw), jnp.bfloat16)],
        compiler_params=_cparams(1),
        name="axial_prep",
    )(proj, proj, proj, g_q2, g_k2, *tables)


DENSE_TILES_PER_STEP = 2


def _dense_kernel(q_ref, k_ref, v_ref, o_ref, *, tq, head_blocks):
    lane = lax.broadcasted_iota(jnp.int32, (1, LANES), 1)
    half0 = lane < HEAD_DIM
    seq = q_ref.shape[0]

    def head_operand(ref, i):
        return ref[:, i * LANES:(i + 1) * LANES] if head_blocks else ref[...]

    def body(t, carry):
        starts = [pl.multiple_of((t * DENSE_TILES_PER_STEP + u) * tq, tq) for u in range(DENSE_TILES_PER_STEP)]
        scores = []
        for qs in starts:
            q = q_ref[pl.ds(qs, tq), :]
            for i in range(2):
                if head_blocks:
                    qi = q[:, i * LANES:(i + 1) * LANES]
                else:
                    qi = jnp.where(half0 if i == 0 else jnp.logical_not(half0), q, jnp.zeros_like(q))
                scores.append(_dot_nt(qi, head_operand(k_ref, i)))
        for u, qs in enumerate(starts):
            outs = []
            for i in range(2):
                s = scores[2 * u + i]
                vi = head_operand(v_ref, i)
                m = jnp.max(s, axis=-1, keepdims=True)
                p = jnp.exp((s - m).astype(vi.dtype))
                pv = jnp.dot(p, vi, preferred_element_type=jnp.float32)
                outs.append(pv * (1.0 / pv[:, DEN_LANE:DEN_LANE + 1]))
            o_ref[pl.ds(qs, tq), :] = jnp.where(half0, outs[0], pltpu.roll(outs[1], HEAD_DIM, 1)).astype(o_ref.dtype)
        return carry

    lax.fori_loop(0, seq // (tq * DENSE_TILES_PER_STEP), body, 0)


def _dense_attention(q, k, v, q_spec, k_spec, v_spec, head_blocks, name, tq=256):
    t = q.shape[0]
    b = t // SEQ
    return pl.pallas_call(
        functools.partial(_dense_kernel, tq=tq, head_blocks=head_blocks),
        grid=(b, N_PAIRS),
        in_specs=[q_spec, k_spec, v_spec],
        out_specs=pl.BlockSpec((SEQ, LANES), lambda i, j: (i, j)),
        out_shape=jax.ShapeDtypeStruct((t, N_PAIRS * LANES), jnp.bfloat16),
        compiler_params=_cparams(2),
        name=name,
    )(q, k, v)


def _merge_kernel(xn_ref, oa_ref, ob_ref, oc_ref, od_ref, wga_ref, wgb_ref, wgc_ref, wgd_ref, bg_ref, wb_ref,
                  out_ref):
    xn = xn_ref[...]
    acc = None
    for j, (o_ref, wg_ref) in enumerate(((oa_ref, wga_ref), (ob_ref, wgb_ref), (oc_ref, wgc_ref), (od_ref, wgd_ref))):
        z = _dot_nt(xn, wg_ref[...]) + bg_ref[j:j + 1, :]
        g = 1.0 / (1.0 + jnp.exp(-z))
        u = jnp.dot(o_ref[...], wb_ref[j], preferred_element_type=jnp.float32)
        acc = g * u if acc is None else acc + g * u
    out_ref[...] = acc.astype(out_ref.dtype)


def _merge(xn, o_list, w_all, b_gate, w_branch, layer, tm=512, tc=512):
    t = xn.shape[0]
    nc = D_MODEL // tc
    gate_blk0 = (N_DIL * GROUP_COLS + REST_COLS) // tc
    o_spec = pl.BlockSpec((tm, BRANCH_WIDTH), lambda c, i: (i, 0))

    def wg_spec(j):
        return pl.BlockSpec((None, tc, D_MODEL), lambda c, i, j=j: (layer, gate_blk0 + j * nc + c, 0))

    return pl.pallas_call(
        _merge_kernel,
        grid=(nc, t // tm),
        in_specs=[pl.BlockSpec((tm, D_MODEL), lambda c, i: (i, 0))] + [o_spec] * 4
        + [wg_spec(0), wg_spec(1), wg_spec(2), wg_spec(3),
           pl.BlockSpec((None, N_BRANCH, tc), lambda c, i: (layer, 0, c)),
           pl.BlockSpec((None, N_BRANCH, BRANCH_WIDTH, tc), lambda c, i: (layer, 0, 0, c))],
        out_specs=pl.BlockSpec((tm, tc), lambda c, i: (i, c)),
        out_shape=jax.ShapeDtypeStruct((t, D_MODEL), jnp.bfloat16),
        compiler_params=_cparams(2),
        name="gated_merge",
    )(xn, *o_list, w_all, w_all, w_all, w_all, b_gate, w_branch)


OUT_PROJ_SUBTILES = 2


def _out_proj_kernel(mixed_ref, w_ref, h_ref, g_ref, wr_hi_ref, wr_lo_ref, h_out, xp_out, probs_out):
    rows = mixed_ref.shape[0] // OUT_PROJ_SUBTILES
    half = D_MODEL // 2
    lane = lax.broadcasted_iota(jnp.int32, (1, LANES), 1)
    for t in range(OUT_PROJ_SUBTILES):
        sl = slice(t * rows, (t + 1) * rows)
        h = h_ref[sl, :] + jnp.dot(mixed_ref[sl, :], w_ref[...], preferred_element_type=jnp.float32)
        h_out[sl, :] = h
        xn = h * lax.rsqrt(jnp.mean(h * h, axis=-1, keepdims=True) + EPS) * g_ref[...]
        xp_out[sl, :] = pltpu.pack_elementwise([xn[:, :half], xn[:, half:]], packed_dtype=jnp.bfloat16)
        x_hi = xn.astype(jnp.bfloat16)
        x_lo = (xn - x_hi.astype(jnp.float32)).astype(jnp.bfloat16)
        logits = (jnp.dot(x_hi, wr_hi_ref[...], preferred_element_type=jnp.float32)
                  + jnp.dot(x_lo, wr_hi_ref[...], preferred_element_type=jnp.float32)
                  + jnp.dot(x_hi, wr_lo_ref[...], preferred_element_type=jnp.float32))
        logits = jnp.where(lane < N_EXPERTS, logits, NEG_INF)
        m = jnp.max(logits, axis=-1, keepdims=True)
        e = jnp.exp(logits - m)
        probs_out[sl, :] = e / jnp.sum(e, axis=-1, keepdims=True)


def _out_proj(mixed, w_out, h, g_ffn, wr_hi, wr_lo, layer, tm=512):
    t = h.shape[0]
    row = lambda w: pl.BlockSpec((tm, w), lambda i: (i, 0))
    lay = lambda a, b: pl.BlockSpec((None, a, b), lambda i: (layer, 0, 0))
    return pl.pallas_call(
        _out_proj_kernel,
        grid=(t // tm,),
        in_specs=[row(D_MODEL), lay(D_MODEL, D_MODEL), row(D_MODEL), lay(1, D_MODEL),
                  lay(D_MODEL, LANES), lay(D_MODEL, LANES)],
        out_specs=[row(D_MODEL), row(D_MODEL // 2), row(LANES)],
        out_shape=[jax.ShapeDtypeStruct((t, D_MODEL), jnp.float32),
                   jax.ShapeDtypeStruct((t, D_MODEL // 2), jnp.uint32),
                   jax.ShapeDtypeStruct((t, LANES), jnp.float32)],
        compiler_params=_cparams(1),
        name="out_proj_router",
    )(mixed, w_out, h, g_ffn, wr_hi, wr_lo)


def _routing_kernel(p_ref, idx_out, gate_out, tri_ref):
    rows, seq = p_ref.shape
    p = p_ref[...]
    tok = lax.broadcasted_iota(jnp.int32, (rows, seq), 1)

    def count(mask):
        return jnp.sum(jnp.where(mask, 1.0, 0.0), axis=-1, keepdims=True)

    def thr_body(i, thr):
        cand = thr | jnp.left_shift(jnp.int32(1), 30 - i)
        cand_f = lax.bitcast_convert_type(cand, jnp.float32)
        return jnp.where(count(p >= cand_f) >= CAPACITY, cand, thr)

    thr = lax.bitcast_convert_type(lax.fori_loop(0, 31, thr_body, jnp.zeros((rows, 1), jnp.int32)), jnp.float32)
    gt = p > thr
    eq = p == thr
    need = CAPACITY - count(gt)

    def tie_body(i, bound):
        cand = bound | jnp.left_shift(jnp.int32(1), 11 - i)
        return jnp.where(count(jnp.logical_and(eq, tok < cand)) < need, cand, bound)

    bound = lax.fori_loop(0, 12, tie_body, jnp.zeros((rows, 1), jnp.int32))
    sel = jnp.logical_or(gt, jnp.logical_and(eq, tok <= bound))

    r_i = lax.broadcasted_iota(jnp.int32, (seq, seq), 0)
    c_i = lax.broadcasted_iota(jnp.int32, (seq, seq), 1)
    tri_ref[...] = jnp.where(r_i < c_i, 1.0, 0.0).astype(jnp.bfloat16)
    self_bf = jnp.where(sel, 1.0, 0.0).astype(jnp.bfloat16)
    pos = jnp.dot(self_bf, tri_ref[...], preferred_element_type=jnp.float32)
    pos = jnp.where(sel, pos, -1.0)

    bf = jnp.bfloat16
    p1 = p.astype(bf).astype(jnp.float32)
    r1 = p - p1
    p2 = r1.astype(bf).astype(jnp.float32)
    p3 = (r1 - p2).astype(bf).astype(jnp.float32)
    t_hi = (tok[0:1] >> 6).astype(jnp.float32)
    t_lo = (tok[0:1] & 63).astype(jnp.float32)
    slot = lax.broadcasted_iota(jnp.int32, (CAPACITY, seq), 0).astype(jnp.float32)
    prow = lax.broadcasted_iota(jnp.int32, (8, seq), 0)
    for r in range(rows):
        onehot = jnp.where(pos[r:r + 1, :] == slot, 1.0, 0.0).astype(bf)
        payload = jnp.where(prow == 0, t_hi, jnp.where(prow == 1, t_lo, jnp.where(
            prow == 2, p1[r:r + 1], jnp.where(prow == 3, p2[r:r + 1], jnp.where(prow == 4, p3[r:r + 1], 0.0)))))
        g = lax.dot_general(payload.astype(bf), onehot, (((1,), (1,)), ((), ())),
                            preferred_element_type=jnp.float32)
        idx_out[r:r + 1, :] = (g[0:1] * 64.0 + g[1:2]).astype(jnp.int32)
        gate_out[r:r + 1, :] = g[2:3] + g[3:4] + g[4:5]


def _routing(probs_t):
    rows, seq = probs_t.shape
    rt = N_EXPERTS
    return pl.pallas_call(
        _routing_kernel,
        grid=(rows // rt,),
        in_specs=[pl.BlockSpec((rt, seq), lambda i: (i, 0))],
        out_specs=[pl.BlockSpec((rt, CAPACITY), lambda i: (i, 0)), pl.BlockSpec((rt, CAPACITY), lambda i: (i, 0))],
        out_shape=[jax.ShapeDtypeStruct((rows, CAPACITY), jnp.int32),
                   jax.ShapeDtypeStruct((rows, CAPACITY), jnp.float32)],
        scratch_shapes=[pltpu.VMEM((seq, seq), jnp.bfloat16)],
        compiler_params=_cparams(1),
        name="expert_choice_routing",
    )(probs_t)


def _gather_kernel(idx_ref, x_ref, o_ref):
    def body(i, c):
        t = idx_ref[0, i]
        o_ref[pl.ds(i, 1), :] = x_ref[pl.ds(t, 1), :]
        return c
    lax.fori_loop(0, CAPACITY, body, 0, unroll=8)


def _gather(idx, x_packed):
    b, e = idx.shape[:2]
    w = x_packed.shape[-1]
    return pl.pallas_call(
        _gather_kernel,
        grid=(b, e),
        in_specs=[pl.BlockSpec((None, None, 1, CAPACITY), lambda i, j: (i, j, 0, 0), memory_space=pltpu.SMEM),
                  pl.BlockSpec((None, SEQ, w), lambda i, j: (i, 0, 0))],
        out_specs=pl.BlockSpec((None, None, CAPACITY, w), lambda i, j: (i, j, 0, 0)),
        out_shape=jax.ShapeDtypeStruct((b, e, CAPACITY, w), jnp.uint32),
        compiler_params=_cparams(2),
        name="moe_gather",
    )(idx, x_packed)


FFN_CHUNKS = 4


def _ffn_kernel(x_ref, gate_ref, wg_ref, wu_ref, wd_ref, o_ref, xs_ref, hs_ref):
    bf = jnp.bfloat16
    f = pl.program_id(2)
    rows = xs_ref.shape[0]

    @pl.when(f == 0)
    def _():
        xp = x_ref[...].reshape(rows, x_ref.shape[2])
        lo = pltpu.unpack_elementwise(xp, index=0, packed_dtype=bf, unpacked_dtype=jnp.float32).astype(bf)
        hi = pltpu.unpack_elementwise(xp, index=1, packed_dtype=bf, unpacked_dtype=jnp.float32).astype(bf)
        xs_ref[...] = jnp.concatenate([lo, hi], axis=1)

    x = xs_ref[...]
    a = jnp.dot(x, wg_ref[...].astype(bf), preferred_element_type=jnp.float32)
    u = jnp.dot(x, wu_ref[...].astype(bf), preferred_element_type=jnp.float32)
    hs_ref[f] = (a / (1.0 + jnp.exp(-a)) * u).astype(bf)

    @pl.when(f == FFN_CHUNKS - 1)
    def _():
        h_all = jnp.concatenate([hs_ref[c] for c in range(FFN_CHUNKS)], axis=1)
        y = jnp.dot(h_all, wd_ref[...].astype(bf), preferred_element_type=jnp.float32)
        out = y * gate_ref[...].reshape(rows, 1)
        o_ref[...] = out.reshape(o_ref.shape).astype(o_ref.dtype)


def _ffn(xe, gate, w_gate, w_up, w_down, layer):
    b, e, c, w = xe.shape
    bb = 4 if b % 4 == 0 else 1
    fc = EXPERT_FF // FFN_CHUNKS
    return pl.pallas_call(
        _ffn_kernel,
        grid=(e, b // bb, FFN_CHUNKS),
        in_specs=[pl.BlockSpec((bb, None, c, w), lambda j, i, f: (i, j, 0, 0)),
                  pl.BlockSpec((bb, None, c, 1), lambda j, i, f: (i, j, 0, 0)),
                  pl.BlockSpec((None, None, D_MODEL, fc), lambda j, i, f: (layer, j, 0, f)),
                  pl.BlockSpec((None, None, D_MODEL, fc), lambda j, i, f: (layer, j, 0, f)),
                  pl.BlockSpec((None, None, EXPERT_FF, D_MODEL), lambda j, i, f: (layer, j, 0, 0))],
        out_specs=pl.BlockSpec((bb, None, c, D_MODEL), lambda j, i, f: (i, j, 0, 0)),
        out_shape=jax.ShapeDtypeStruct((b, e, c, D_MODEL), jnp.bfloat16),
        scratch_shapes=[pltpu.VMEM((bb * c, D_MODEL), jnp.bfloat16),
                        pltpu.VMEM((FFN_CHUNKS, bb * c, fc), jnp.bfloat16)],
        compiler_params=_cparams(3),
        name="expert_ffn",
    )(xe, gate, w_gate, w_up, w_down)


def _combine_kernel(idx_ref, y_ref, h_ref, g_ref, *out_refs, final):
    tm = h_ref.shape[0]
    tok = pl.program_id(1) * tm + lax.broadcasted_iota(jnp.int32, (tm, idx_ref.shape[1]), 0)
    sel = jnp.where(tok == idx_ref[...], 1.0, 0.0).astype(y_ref.dtype)
    h = h_ref[...] + jnp.dot(sel, y_ref[...], preferred_element_type=jnp.float32)
    xn = h * lax.rsqrt(jnp.mean(h * h, axis=-1, keepdims=True) + EPS) * g_ref[...]
    if final:
        out_refs[0][...] = xn.astype(out_refs[0].dtype)
    else:
        out_refs[0][...] = h
        out_refs[1][...] = xn.astype(out_refs[1].dtype)


def _combine(idx, ye, h, g_next, final, tm=256):
    b, n_slots = idx.shape[0], idx.shape[-1]
    row = pl.BlockSpec((None, tm, D_MODEL), lambda i, r: (i, r, 0))
    shape = lambda dt: jax.ShapeDtypeStruct((b, SEQ, D_MODEL), dt)
    return pl.pallas_call(
        functools.partial(_combine_kernel, final=final),
        grid=(b, SEQ // tm),
        in_specs=[pl.BlockSpec((None, 1, n_slots), lambda i, r: (i, 0, 0)),
                  pl.BlockSpec((None, n_slots, D_MODEL), lambda i, r: (i, 0, 0)),
                  row,
                  pl.BlockSpec((1, D_MODEL), lambda i, r: (0, 0))],
        out_specs=[row] if final else [row, row],
        out_shape=[shape(jnp.float32)] if final else [shape(jnp.float32), shape(jnp.bfloat16)],
        compiler_params=_cparams(2),
        name="moe_combine",
    )(idx, ye, h, g_next.reshape(1, D_MODEL))


def _alibi_slopes(n):
    return jnp.exp2(-8.0 * jnp.arange(1, n + 1, dtype=jnp.float32) / n)


def _token_mixer(xn, l, p, tabs):
    t = xn.shape[0]
    b = t // SEQ
    bf = jnp.bfloat16
    groups = _proj_dilated(xn, p["w_all"], l)
    proj = _proj_rest(xn, p["w_all"], l, tm=min(t, 2048))
    proj4 = proj.reshape(b, 1, SEQ, REST_COLS)

    dils = jnp.asarray([dil for _, dil in A_DILATIONS], jnp.float32)
    o_a = _dilated_attention(groups, _alibi_slopes(A_HEADS)[None, :] * dils[:, None])

    qb, kb, vb = _mla_prep(proj, p["g_mla_q"][l], p["g_mla_kv"][l], p["w_mla_uq"][l], p["w_mla_ukv"][l],
                           tabs["mla"], (MLA_NOPE + MLA_ROPE) ** -0.5)
    o_b = _dense_attention(qb, kb, vb,
                           pl.BlockSpec((SEQ, 2 * LANES), lambda i, j: (i, j)),
                           pl.BlockSpec((SEQ, 2 * LANES), lambda i, j: (i, j)),
                           pl.BlockSpec((SEQ, 2 * LANES), lambda i, j: (i, j)), True, "latent_attention")

    qc, kc, vc = _c_prep(proj, p["g_c_q"][l], p["g_c_k"][l], tabs["axial"], HEAD_DIM ** -0.5)
    o_c = _dense_attention(qc, kc, vc,
                           pl.BlockSpec((SEQ, LANES), lambda i, j: (i, j)),
                           pl.BlockSpec((SEQ, LANES), lambda i, j: (i, j // 2)),
                           pl.BlockSpec((SEQ, LANES), lambda i, j: (i, j // 2)), False, "axial_attention")

    o_d = _sink_attention(proj4, _alibi_slopes(D_Q_HEADS), p["sink_d"][l].astype(jnp.float32))

    return _merge(xn, (o_a, o_b, o_c, o_d), p["w_all"], p["b_gate"], p["w_branch"], l)


def _layer(h, xn, l, p, tabs, g_next, final):
    t = h.shape[0]
    b = t // SEQ
    mixed = _token_mixer(xn, l, p, tabs)
    h, x_packed, probs = _out_proj(mixed, p["w_out"], h, p["g_ffn_norm"], p["wr_hi"], p["wr_lo"], l)
    probs_t = jnp.transpose(probs.reshape(b, SEQ, LANES)[:, :, :N_EXPERTS], (0, 2, 1)).reshape(b * N_EXPERTS, SEQ)
    idx, gate = _routing(probs_t)
    xe = _gather(idx.reshape(b, N_EXPERTS, 1, CAPACITY), x_packed.reshape(b, SEQ, D_MODEL // 2))
    ye = _ffn(xe, gate.reshape(b, N_EXPERTS, CAPACITY, 1), p["w_exp_gate"], p["w_exp_up"], p["w_exp_down"], l)
    outs = _combine(idx.reshape(b, 1, N_EXPERTS * CAPACITY), ye.reshape(b, N_EXPERTS * CAPACITY, D_MODEL),
                    h.reshape(b, SEQ, D_MODEL), g_next, final)
    return [o.reshape(t, D_MODEL) for o in outs]


def _prepare_params(w_in, w_mla_uq, w_mla_ukv, w_branch, w_out, w_router, w_exp_gate, w_exp_up, w_exp_down):
    bf = jnp.bfloat16
    p = {}
    p["w_all"] = _relayout_w_in(w_in)
    p["w_mla_uq"] = jnp.stack([_mla_uq_layout(w_mla_uq[l]) for l in range(DEPTH)]).astype(bf)
    p["w_mla_ukv"] = jnp.stack([_mla_ukv_layout(w_mla_ukv[l]) for l in range(DEPTH)]).astype(bf)
    p["w_branch"] = w_branch.astype(bf)
    p["w_out"] = w_out.astype(bf)
    wr = jnp.pad(w_router, ((0, 0), (0, 0), (0, LANES - N_EXPERTS)))
    p["wr_hi"] = wr.astype(bf)
    p["wr_lo"] = (wr - p["wr_hi"].astype(jnp.float32)).astype(bf)
    p["w_exp_gate"] = w_exp_gate
    p["w_exp_up"] = w_exp_up
    p["w_exp_down"] = w_exp_down
    return p


def kernel(x, w_in, b_gate, g_attn_norm, g_ffn_norm, g_mla_q, g_mla_kv, w_mla_uq, w_mla_ukv, g_c_q, g_c_k, sink_d,
           w_branch, w_out, w_router, w_exp_gate, w_exp_up, w_exp_down, g_final):
    b, s, d = x.shape
    assert (s, d) == (SEQ, D_MODEL)
    p = _prepare_params(w_in, w_mla_uq, w_mla_ukv, w_branch, w_out, w_router, w_exp_gate, w_exp_up, w_exp_down)
    p.update(b_gate=b_gate, g_attn_norm=g_attn_norm, g_ffn_norm=g_ffn_norm.reshape(DEPTH, 1, D_MODEL),
             g_mla_q=g_mla_q, g_mla_kv=g_mla_kv, g_c_q=g_c_q, g_c_k=g_c_k, sink_d=sink_d)
    pos = np.arange(SEQ)
    rows, cols = pos // GRID_W, pos % GRID_W
    tabs = {
        "mla": _rope_tables([pos], MLA_ROPE, [MLA_NOPE]),
        "axial": _rope_tables([rows, cols, rows, cols], HEAD_DIM // 2, [0, 32, 64, 96]),
    }
    h = x.reshape(b * s, d)
    xn = _rmsnorm(h, g_attn_norm[0], jnp.bfloat16)
    for l in range(DEPTH - 1):
        h, xn = _layer(h, xn, l, p, tabs, g_attn_norm[l + 1], False)
    (out,) = _layer(h, xn, DEPTH - 1, p, tabs, g_final, True)
    return out.reshape(b, s, d)
```

```python
import functools

import numpy as np
import jax
import jax.numpy as jnp
from jax import lax
from jax.experimental import pallas as pl
from jax.experimental.pallas import tpu as pltpu

D_MODEL = 2048
SEQ = 2048
DEPTH = 2
HEAD_DIM = 64
GRID_W = 64
ROPE_THETA = 10000.0
NEG_INF = -1e30
EPS = 1e-6
N_BRANCH = 4
BRANCH_WIDTH = 512

A_HEADS = 8
A_DILATIONS = ((128, 1), (512, 4), (2048, 16))
N_DIL = 3
MLA_HEADS = 8
MLA_Q_LORA = 512
MLA_KV_LORA = 256
MLA_NOPE = 64
MLA_ROPE = 32
MLA_V = 64
C_Q_HEADS = 8
C_KV_HEADS = 2
D_Q_HEADS = 8
D_KV_HEADS = 2
D_HALF_WINDOW = 128

A_COLS = 3 * N_DIL * A_HEADS * HEAD_DIM
B_COLS = MLA_Q_LORA + MLA_KV_LORA + MLA_ROPE
C_COLS = (C_Q_HEADS + 2 * C_KV_HEADS) * HEAD_DIM
D_COLS = (D_Q_HEADS + 2 * D_KV_HEADS) * HEAD_DIM
GATE_COL0 = A_COLS + B_COLS + C_COLS + D_COLS

N_EXPERTS = 16
EXPERT_FF = D_MODEL // 2
CAPACITY = 2 * SEQ // N_EXPERTS

LANES = 128
N_PAIRS = 4
DEN_LANE = HEAD_DIM
VMEM_LIMIT = 56 * 1024 * 1024

GROUP_COLS = 3 * A_HEADS * HEAD_DIM
REST_COLS = 3072
GATE_COLS = N_BRANCH * D_MODEL
BLK_CQ = 0
BLK_QC = 4
BLK_QD = 8
BLK_CKV = 12
BLK_KPE = 14
BLK_KC = 16
BLK_VC = 18
BLK_KD = 20
BLK_VD = 22


def _cparams(n_grid_dims):
    return pltpu.CompilerParams(dimension_semantics=("arbitrary",) * n_grid_dims, vmem_limit_bytes=VMEM_LIMIT)


def _proj_segments():
    segs = [(0, A_COLS)]
    b0 = A_COLS
    c0 = A_COLS + B_COLS
    d0 = c0 + C_COLS
    segs.append((b0, MLA_Q_LORA))
    segs.append((c0, C_Q_HEADS * HEAD_DIM))
    segs.append((d0, D_Q_HEADS * HEAD_DIM))
    segs.append((b0 + MLA_Q_LORA, MLA_KV_LORA))
    segs += [(None, 64), (b0 + MLA_Q_LORA + MLA_KV_LORA, MLA_ROPE), (None, 32 + LANES)]
    for base, nq, nkv in ((c0, C_Q_HEADS, C_KV_HEADS), (d0, D_Q_HEADS, D_KV_HEADS)):
        for part in range(2):
            for kv in range(nkv):
                start = base + (nq + part * nkv + kv) * HEAD_DIM
                segs += [(start, HEAD_DIM), (start, HEAD_DIM)]
    assert sum(w for _, w in segs) == N_DIL * GROUP_COLS + REST_COLS
    segs.append((GATE_COL0, GATE_COLS))
    return segs


def _relayout_w_in(w_in):
    wt = jnp.transpose(w_in, (0, 2, 1))
    parts = []
    for start, width in _proj_segments():
        if start is None:
            parts.append(jnp.zeros((wt.shape[0], width, wt.shape[2]), jnp.bfloat16))
        else:
            parts.append(wt[:, start:start + width, :].astype(jnp.bfloat16))
    return jnp.concatenate(parts, axis=1)


def _dot_nt(x, w):
    return lax.dot_general(x, w, (((1,), (1,)), ((), ())), preferred_element_type=jnp.float32)


def _mla_uq_layout(w_uq):
    w = w_uq.reshape(MLA_Q_LORA, MLA_HEADS, MLA_NOPE + MLA_ROPE)
    w = jnp.pad(w, ((0, 0), (0, 0), (0, LANES - MLA_NOPE - MLA_ROPE)))
    return w.reshape(MLA_Q_LORA, MLA_HEADS * LANES)


def _mla_ukv_layout(w_ukv):
    w = w_ukv.reshape(MLA_KV_LORA, MLA_HEADS, MLA_NOPE + MLA_V)
    wk = jnp.pad(w[:, :, :MLA_NOPE], ((0, 0), (0, 0), (0, LANES - MLA_NOPE)))
    wv = jnp.pad(w[:, :, MLA_NOPE:], ((0, 0), (0, 0), (0, LANES - MLA_V)))
    return jnp.concatenate([wk.reshape(MLA_KV_LORA, MLA_HEADS * LANES),
                            wv.reshape(MLA_KV_LORA, MLA_HEADS * LANES)], axis=1)


def _rope_tables(pos_list, dim, lane_offsets):
    half = dim // 2
    freqs = ROPE_THETA ** (-np.arange(0, dim, 2, dtype=np.float64) / dim)
    cos_t = np.ones((SEQ, LANES), np.float64)
    sin_a = np.zeros((SEQ, LANES), np.float64)
    sin_b = np.zeros((SEQ, LANES), np.float64)
    for pos, off in zip(pos_list, lane_offsets):
        ang = np.asarray(pos, np.float64)[:, None] * freqs[None, :]
        c, s = np.cos(ang), np.sin(ang)
        cos_t[:, off:off + half] = c
        cos_t[:, off + half:off + dim] = c
        sin_a[:, off:off + half] = -s
        sin_b[:, off + half:off + dim] = s
    return tuple(jnp.asarray(t, jnp.float32) for t in (cos_t, sin_a, sin_b))


def _apply_rope(x, cos_t, sin_a, sin_b, half):
    n = x.shape[-1]
    return (x * cos_t + pltpu.roll(x, n - half, 1) * sin_a + pltpu.roll(x, half, 1) * sin_b)


def _rmsnorm_kernel(x_ref, g_ref, o_ref):
    x = x_ref[...]
    ms = jnp.mean(x * x, axis=-1, keepdims=True)
    o_ref[...] = (x * lax.rsqrt(ms + EPS) * g_ref[...]).astype(o_ref.dtype)


def _rmsnorm(x, g, out_dtype, tm=512):
    t, d = x.shape
    return pl.pallas_call(
        _rmsnorm_kernel,
        grid=(t // tm,),
        in_specs=[pl.BlockSpec((tm, d), lambda i: (i, 0)), pl.BlockSpec((1, d), lambda i: (0, 0))],
        out_specs=pl.BlockSpec((tm, d), lambda i: (i, 0)),
        out_shape=jax.ShapeDtypeStruct((t, d), out_dtype),
        compiler_params=_cparams(1),
        name="rmsnorm",
    )(x, g.reshape(1, d))


def _matmul_kernel(x_ref, w_ref, o_ref):
    o_ref[...] = _dot_nt(x_ref[...], w_ref[...]).astype(o_ref.dtype)


def _proj_rest(x, w_all, layer, tm, tn=1536):
    m, k = x.shape
    blk0 = N_DIL * GROUP_COLS // tn
    return pl.pallas_call(
        _matmul_kernel,
        grid=(m // tm, REST_COLS // tn),
        in_specs=[pl.BlockSpec((tm, k), lambda i, j: (i, 0)),
                  pl.BlockSpec((None, tn, k), lambda i, j: (layer, blk0 + j, 0))],
        out_specs=pl.BlockSpec((tm, tn), lambda i, j: (i, j)),
        out_shape=jax.ShapeDtypeStruct((m, REST_COLS), jnp.bfloat16),
        compiler_params=_cparams(2),
        name="in_proj",
    )(x, w_all)


def _proj_dilated_kernel(x_ref, w0_ref, w1_ref, w2_ref, o0_ref, o1_ref, o2_ref, scr_ref):
    x = x_ref[...]
    for w_ref, o_ref, (_, dil) in zip((w0_ref, w1_ref, w2_ref), (o0_ref, o1_ref, o2_ref), A_DILATIONS):
        acc = _dot_nt(x, w_ref[...])
        if dil == 1:
            o_ref[0] = acc.astype(o_ref.dtype)
        else:
            n_blk = acc.shape[1] // LANES
            for c in range(n_blk):
                scr_ref[c] = acc[:, c * LANES:(c + 1) * LANES]
            for r in range(dil):
                for c in range(n_blk):
                    rows = scr_ref[c, pl.ds(r, SEQ // dil, stride=dil), :]
                    o_ref[r, :, c * LANES:(c + 1) * LANES] = rows.astype(o_ref.dtype)


def _proj_dilated(x, w_all, layer, tn=512):
    t, k = x.shape
    b = t // SEQ
    nc = GROUP_COLS // tn

    def w_spec(g):
        return pl.BlockSpec((None, tn, k), lambda i, c, g=g: (layer, c * N_DIL + g, 0))

    out_specs, out_shape = [], []
    for _, dil in A_DILATIONS:
        ld = SEQ // dil
        out_specs.append(pl.BlockSpec((None, dil, ld, tn), lambda i, c: (i, 0, 0, c)))
        out_shape.append(jax.ShapeDtypeStruct((b, dil, ld, GROUP_COLS), jnp.bfloat16))
    return pl.pallas_call(
        _proj_dilated_kernel,
        grid=(b, nc),
        in_specs=[pl.BlockSpec((SEQ, k), lambda i, c: (i, 0)), w_spec(0), w_spec(1), w_spec(2)],
        out_specs=out_specs,
        out_shape=out_shape,
        scratch_shapes=[pltpu.VMEM((tn // LANES, SEQ, LANES), jnp.float32)],
        compiler_params=_cparams(2),
        name="in_proj_dilated",
    )(x, w_all, w_all, w_all)


BANDED_GROUP_SCORE_ELEMS = 256 * 1024


def _banded_tiles(half_window, length):
    return 2 * half_window, min(4 * half_window, length)


def _banded_sequences(slopes, sinks, q_ref, k_ref, v_ref, bias_ref, write, *, length, half_window, dil):
    use_sink = sinks is not None
    w = half_window
    tq, tk = _banded_tiles(w, length)
    nq = length // tq
    key_start = [min(max(qb * tq - w, 0), length - tk) for qb in range(nq)]
    offsets = sorted({key_start[qb] - qb * tq for qb in range(nq)})

    lane = lax.broadcasted_iota(jnp.int32, (1, LANES), 1)
    half0 = lane < HEAD_DIM
    delta = lax.broadcasted_iota(jnp.int32, (tq, tk), 1) - lax.broadcasted_iota(jnp.int32, (tq, tk), 0)
    for oi, off in enumerate(offsets):
        rel = jnp.abs(delta + off)
        dist = rel.astype(jnp.float32)
        for i in range(2):
            bias_ref[2 * oi + i] = jnp.where(rel <= w, -slopes[i] * dist, NEG_INF)

    scale = jnp.asarray(HEAD_DIM ** -0.5, q_ref.dtype)
    blocks = [(r, qb) for r in range(dil) for qb in range(nq)]
    group = max(1, BANDED_GROUP_SCORE_ELEMS // (2 * tq * tk))
    for g0 in range(0, len(blocks), group):
        grp = blocks[g0:g0 + group]
        scores = []
        for r, qb in grp:
            qs, ks = qb * tq, key_start[qb]
            oi = offsets.index(ks - qs)
            q = q_ref[r, qs:qs + tq, :] * scale
            kw = k_ref[r, ks:ks + tk, :]
            for i in range(2):
                qi = jnp.where(half0 if i == 0 else jnp.logical_not(half0), q, jnp.zeros_like(q))
                scores.append(_dot_nt(qi, kw) + bias_ref[2 * oi + i])
        probs, maxes, dens = [], [], []
        for n, s in enumerate(scores):
            m = jnp.max(s, axis=-1, keepdims=True)
            if use_sink:
                sk = sinks[n % 2]
                m = jnp.maximum(m, sk)
            p = jnp.exp(s - m)
            den = jnp.sum(p, axis=-1, keepdims=True)
            if use_sink:
                den = den + jnp.exp(sk - m)
            probs.append(p.astype(v_ref.dtype))
            maxes.append(m)
            dens.append(den)
        for n, (r, qb) in enumerate(grp):
            qs, ks = qb * tq, key_start[qb]
            vw = v_ref[r, ks:ks + tk, :]
            outs = [jnp.dot(probs[2 * n + i], vw, preferred_element_type=jnp.float32) / dens[2 * n + i]
                    for i in range(2)]
            rows = pl.ds(qs, tq) if dil == 1 else pl.ds(r + qs * dil, tq, stride=dil)
            lses = [maxes[2 * n + i] + jnp.log(dens[2 * n + i]) for i in range(2)]
            write(rows, jnp.where(half0, outs[0], outs[1]), jnp.where(half0, lses[0], lses[1]))


def _bias_scratch(half_window, length):
    tq, tk = _banded_tiles(half_window, length)
    return pltpu.VMEM((2 if tk == length else 6, tq, tk), jnp.float32)


def _sink_kernel(slope_ref, sink_ref, q_ref, k_ref, v_ref, o_ref, bias_ref):
    hp = pl.program_id(1)

    def write(rows, o, lse):
        del lse
        o_ref[rows, :] = o.astype(o_ref.dtype)

    _banded_sequences([slope_ref[2 * hp + i] for i in range(2)], [sink_ref[2 * hp + i] for i in range(2)],
                      q_ref, k_ref, v_ref, bias_ref, write, length=SEQ, half_window=D_HALF_WINDOW, dil=1)


def _sink_attention(proj4, slopes, sinks):
    b = proj4.shape[0]
    smem = pl.BlockSpec(memory_space=pltpu.SMEM)

    def spec(blk_fn):
        return pl.BlockSpec((None, 1, SEQ, LANES), lambda i, j: (i, 0, 0, blk_fn(j)))

    return pl.pallas_call(
        _sink_kernel,
        grid=(b, N_PAIRS),
        in_specs=[smem, smem, spec(lambda j: BLK_QD + j), spec(lambda j: BLK_KD + j // 2),
                  spec(lambda j: BLK_VD + j // 2)],
        out_specs=pl.BlockSpec((SEQ, LANES), lambda i, j: (i, j)),
        out_shape=jax.ShapeDtypeStruct((b * SEQ, N_PAIRS * LANES), jnp.bfloat16),
        scratch_shapes=[_bias_scratch(D_HALF_WINDOW, SEQ)],
        compiler_params=_cparams(2),
        name="banded_sink",
    )(slopes, sinks, proj4, proj4, proj4)


def _dilated_kernel(slope_ref, *refs):
    qkv_refs = refs[:9]
    o_ref = refs[9]
    bias_refs = refs[10:13]
    o_stage, lse_stage = refs[13], refs[14]
    hp = pl.program_id(1)
    for g, (window, dil) in enumerate(A_DILATIONS):
        def write(rows, o, lse, g=g):
            o_stage[g, rows, :] = o
            lse_stage[g, rows, :] = lse

        _banded_sequences([slope_ref[g, 2 * hp + i] for i in range(2)], None, *qkv_refs[3 * g:3 * g + 3],
                          bias_refs[g], write, length=SEQ // dil, half_window=(window // 2) // dil, dil=dil)
    la, lb, lc = lse_stage[0], lse_stage[1], lse_stage[2]
    m = jnp.maximum(jnp.maximum(la, lb), lc)
    ea, eb, ec = jnp.exp(la - m), jnp.exp(lb - m), jnp.exp(lc - m)
    num = ea * o_stage[0] + eb * o_stage[1] + ec * o_stage[2]
    o_ref[...] = (num / (ea + eb + ec)).astype(o_ref.dtype)


def _dilated_attention(groups, slopes):
    b = groups[0].shape[0]
    in_specs = [pl.BlockSpec(memory_space=pltpu.SMEM)]
    operands = [slopes]
    scratch = []
    for g, (window, dil) in enumerate(A_DILATIONS):
        ld = SEQ // dil
        for s in range(3):
            in_specs.append(pl.BlockSpec((None, dil, ld, LANES), lambda i, j, s=s: (i, 0, 0, s * N_PAIRS + j)))
            operands.append(groups[g])
        scratch.append(_bias_scratch((window // 2) // dil, ld))
    scratch += [pltpu.VMEM((N_DIL, SEQ, LANES), jnp.float32)] * 2
    return pl.pallas_call(
        _dilated_kernel,
        grid=(b, N_PAIRS),
        in_specs=in_specs,
        out_specs=pl.BlockSpec((SEQ, LANES), lambda i, j: (i, j)),
        out_shape=jax.ShapeDtypeStruct((b * SEQ, N_PAIRS * LANES), jnp.bfloat16),
        scratch_shapes=scratch,
        compiler_params=_cparams(2),
        name="banded_dilated",
    )(*operands)


def _mla_prep_kernel(cq_ref, ckv_ref, kpe_ref, gq_ref, gkv_ref, wuq_ref, wukv_ref, cos_ref, sa_ref, sb_ref,
                     q_out, k_out, v_out, *, scale):
    def norm(x, g):
        x = x.astype(jnp.float32)
        return (x * lax.rsqrt(jnp.mean(x * x, axis=-1, keepdims=True) + EPS) * g).astype(jnp.bfloat16)

    cos_t, sin_a, sin_b = cos_ref[...], sa_ref[...], sb_ref[...]
    half = MLA_ROPE // 2
    q = jnp.dot(norm(cq_ref[...], gq_ref[...]), wuq_ref[...], preferred_element_type=jnp.float32)
    kv = jnp.dot(norm(ckv_ref[...], gkv_ref[...]), wukv_ref[...], preferred_element_type=jnp.float32)
    kpe = _apply_rope(kpe_ref[...].astype(jnp.float32), cos_t, sin_a, sin_b, half)
    for h in range(MLA_HEADS):
        sl = slice(h * LANES, (h + 1) * LANES)
        q_out[:, sl] = (_apply_rope(q[:, sl], cos_t, sin_a, sin_b, half) * scale).astype(q_out.dtype)
        k_out[:, sl] = (kv[:, sl] + kpe).astype(k_out.dtype)
    lane = lax.broadcasted_iota(jnp.int32, (1, MLA_HEADS * LANES), 1)
    ones_lane = jnp.where(lane % LANES == DEN_LANE, 1.0, 0.0)
    v_out[...] = (kv[:, MLA_HEADS * LANES:] + ones_lane).astype(v_out.dtype)


def _mla_prep(proj, g_q, g_kv, w_uq, w_ukv, tables, scale, tm=512):
    t = proj.shape[0]
    nblk = SEQ // tm
    tab_spec = pl.BlockSpec((tm, LANES), lambda i: (i % nblk, 0))
    nq, nk, nv = MLA_HEADS * LANES, MLA_HEADS * LANES, MLA_HEADS * LANES
    return pl.pallas_call(
        functools.partial(_mla_prep_kernel, scale=scale),
        grid=(t // tm,),
        in_specs=[pl.BlockSpec((tm, MLA_Q_LORA), lambda i: (i, BLK_CQ * LANES // MLA_Q_LORA)),
                  pl.BlockSpec((tm, MLA_KV_LORA), lambda i: (i, BLK_CKV * LANES // MLA_KV_LORA)),
                  pl.BlockSpec((tm, LANES), lambda i: (i, BLK_KPE)),
                  pl.BlockSpec((1, MLA_Q_LORA), lambda i: (0, 0)),
                  pl.BlockSpec((1, MLA_KV_LORA), lambda i: (0, 0)),
                  pl.BlockSpec(w_uq.shape, lambda i: (0, 0)),
                  pl.BlockSpec(w_ukv.shape, lambda i: (0, 0)),
                  tab_spec, tab_spec, tab_spec],
        out_specs=[pl.BlockSpec((tm, nq), lambda i: (i, 0)),
                   pl.BlockSpec((tm, nk), lambda i: (i, 0)),
                   pl.BlockSpec((tm, nv), lambda i: (i, 0))],
        out_shape=[jax.ShapeDtypeStruct((t, nq), jnp.bfloat16),
                   jax.ShapeDtypeStruct((t, nk), jnp.bfloat16),
                   jax.ShapeDtypeStruct((t, nv), jnp.bfloat16)],
        compiler_params=_cparams(1),
        name="mla_prep",
    )(proj, proj, proj, g_q.reshape(1, -1), g_kv.reshape(1, -1), w_uq, w_ukv, *tables)


def _c_prep_kernel(q_ref, k_ref, v_ref, gq_ref, gk_ref, cos_ref, sa_ref, sb_ref, q_out, k_out, v_out, *, scale):
    lane = lax.broadcasted_iota(jnp.int32, (1, LANES), 1)
    half0 = lane < HEAD_DIM
    cos_t, sin_a, sin_b = cos_ref[...], sa_ref[...], sb_ref[...]
    ones_lane = jnp.where(lane == DEN_LANE, 1.0, 0.0).astype(v_out.dtype)
    for p in range(v_ref.shape[1] // LANES):
        sl = slice(p * LANES, (p + 1) * LANES)
        v_out[:, sl] = jnp.where(half0, v_ref[:, sl], ones_lane)

    def prep(x, g):
        x = x.astype(jnp.float32)
        sq = x * x
        s0 = jnp.sum(jnp.where(half0, sq, 0.0), axis=-1, keepdims=True)
        s1 = jnp.sum(jnp.where(half0, 0.0, sq), axis=-1, keepdims=True)
        inv = jnp.where(half0, lax.rsqrt(s0 / HEAD_DIM + EPS), lax.rsqrt(s1 / HEAD_DIM + EPS))
        return _apply_rope(x * inv * g, cos_t, sin_a, sin_b, HEAD_DIM // 4)

    for p in range(q_ref.shape[1] // LANES):
        sl = slice(p * LANES, (p + 1) * LANES)
        q_out[:, sl] = (prep(q_ref[:, sl], gq_ref[...]) * scale).astype(q_out.dtype)
    for p in range(k_ref.shape[1] // LANES):
        sl = slice(p * LANES, (p + 1) * LANES)
        k_out[:, sl] = prep(k_ref[:, sl], gk_ref[...]).astype(k_out.dtype)


def _c_prep(proj, g_q, g_k, tables, scale, tm=512):
    t = proj.shape[0]
    nblk = SEQ // tm
    tab_spec = pl.BlockSpec((tm, LANES), lambda i: (i % nblk, 0))
    qw, kw = N_PAIRS * LANES, C_KV_HEADS * LANES
    g_q2 = jnp.tile(g_q.reshape(1, HEAD_DIM), (1, 2))
    g_k2 = jnp.tile(g_k.reshape(1, HEAD_DIM), (1, 2))
    return pl.pallas_call(
        functools.partial(_c_prep_kernel, scale=scale),
        grid=(t // tm,),
        in_specs=[pl.BlockSpec((tm, qw), lambda i: (i, BLK_QC * LANES // qw)),
                  pl.BlockSpec((tm, kw), lambda i: (i, BLK_KC * LANES // kw)),
                  pl.BlockSpec((tm, kw), lambda i: (i, BLK_VC * LANES // kw)),
                  pl.BlockSpec((1, LANES), lambda i: (0, 0)),
                  pl.BlockSpec((1, LANES), lambda i: (0, 0)),
                  tab_spec, tab_spec, tab_spec],
        out_specs=[pl.BlockSpec((tm, qw), lambda i: (i, 0)), pl.BlockSpec((tm, kw), lambda i: (i, 0)),
                   pl.BlockSpec((tm, kw), lambda i: (i, 0))],
        out_shape=[jax.ShapeDtypeStruct((t, qw), jnp.bfloat16), jax.ShapeDtypeStruct((t, kw), jnp.bfloat16),
                   jax.ShapeDtypeStruct((t, kw), jnp.bfloat16)],
        compiler_params=_cparams(1),
        name="axial_prep",
    )(proj, proj, proj, g_q2, g_k2, *tables)


DENSE_TILES_PER_STEP = 8


def _dense_kernel(q_ref, k_ref, v_ref, o_ref, *, tq, head_blocks):
    lane = lax.broadcasted_iota(jnp.int32, (1, LANES), 1)
    half0 = lane < HEAD_DIM
    seq = q_ref.shape[0]

    def head_operand(ref, i):
        return ref[:, i * LANES:(i + 1) * LANES] if head_blocks else ref[...]

    def body(t, carry):
        starts = [pl.multiple_of((t * DENSE_TILES_PER_STEP + u) * tq, tq) for u in range(DENSE_TILES_PER_STEP)]
        scores = []
        for qs in starts:
            q = q_ref[pl.ds(qs, tq), :]
            for i in range(2):
                if head_blocks:
                    qi = q[:, i * LANES:(i + 1) * LANES]
                else:
                    qi = jnp.where(half0 if i == 0 else jnp.logical_not(half0), q, jnp.zeros_like(q))
                scores.append(_dot_nt(qi, head_operand(k_ref, i)))
        for u, qs in enumerate(starts):
            outs = []
            for i in range(2):
                s = scores[2 * u + i]
                vi = head_operand(v_ref, i)
                m = jnp.max(s, axis=-1, keepdims=True)
                p = jnp.exp((s - m).astype(vi.dtype))
                pv = jnp.dot(p, vi, preferred_element_type=jnp.float32)
                outs.append(pv * (1.0 / pv[:, DEN_LANE:DEN_LANE + 1]))
            o_ref[pl.ds(qs, tq), :] = jnp.where(half0, outs[0], pltpu.roll(outs[1], HEAD_DIM, 1)).astype(o_ref.dtype)
        return carry

    lax.fori_loop(0, seq // (tq * DENSE_TILES_PER_STEP), body, 0)


def _dense_attention(q, k, v, q_spec, k_spec, v_spec, head_blocks, name, tq=256):
    t = q.shape[0]
    b = t // SEQ
    return pl.pallas_call(
        functools.partial(_dense_kernel, tq=tq, head_blocks=head_blocks),
        grid=(b, N_PAIRS),
        in_specs=[q_spec, k_spec, v_spec],
        out_specs=pl.BlockSpec((SEQ, LANES), lambda i, j: (i, j)),
        out_shape=jax.ShapeDtypeStruct((t, N_PAIRS * LANES), jnp.bfloat16),
        compiler_params=_cparams(2),
        name=name,
    )(q, k, v)


def _merge_kernel(xn_ref, oa_ref, ob_ref, oc_ref, od_ref, wga_ref, wgb_ref, wgc_ref, wgd_ref, bg_ref, wb_ref,
                  out_ref):
    xn = xn_ref[...]
    acc = None
    for j, (o_ref, wg_ref) in enumerate(((oa_ref, wga_ref), (ob_ref, wgb_ref), (oc_ref, wgc_ref), (od_ref, wgd_ref))):
        z = _dot_nt(xn, wg_ref[...]) + bg_ref[j:j + 1, :]
        g = 1.0 / (1.0 + jnp.exp(-z))
        u = jnp.dot(o_ref[...], wb_ref[j], preferred_element_type=jnp.float32)
        acc = g * u if acc is None else acc + g * u
    out_ref[...] = acc.astype(out_ref.dtype)


def _merge(xn, o_list, w_all, b_gate, w_branch, layer, tm=512, tc=512):
    t = xn.shape[0]
    nc = D_MODEL // tc
    gate_blk0 = (N_DIL * GROUP_COLS + REST_COLS) // tc
    o_spec = pl.BlockSpec((tm, BRANCH_WIDTH), lambda c, i: (i, 0))

    def wg_spec(j):
        return pl.BlockSpec((None, tc, D_MODEL), lambda c, i, j=j: (layer, gate_blk0 + j * nc + c, 0))

    return pl.pallas_call(
        _merge_kernel,
        grid=(nc, t // tm),
        in_specs=[pl.BlockSpec((tm, D_MODEL), lambda c, i: (i, 0))] + [o_spec] * 4
        + [wg_spec(0), wg_spec(1), wg_spec(2), wg_spec(3),
           pl.BlockSpec((None, N_BRANCH, tc), lambda c, i: (layer, 0, c)),
           pl.BlockSpec((None, N_BRANCH, BRANCH_WIDTH, tc), lambda c, i: (layer, 0, 0, c))],
        out_specs=pl.BlockSpec((tm, tc), lambda c, i: (i, c)),
        out_shape=jax.ShapeDtypeStruct((t, D_MODEL), jnp.bfloat16),
        compiler_params=_cparams(2),
        name="gated_merge",
    )(xn, *o_list, w_all, w_all, w_all, w_all, b_gate, w_branch)


OUT_PROJ_SUBTILES = 2


def _out_proj_kernel(mixed_ref, w_ref, h_ref, g_ref, wr_hi_ref, wr_lo_ref, h_out, xp_out, probs_out):
    rows = mixed_ref.shape[0] // OUT_PROJ_SUBTILES
    half = D_MODEL // 2
    lane = lax.broadcasted_iota(jnp.int32, (1, LANES), 1)
    for t in range(OUT_PROJ_SUBTILES):
        sl = slice(t * rows, (t + 1) * rows)
        h = h_ref[sl, :] + jnp.dot(mixed_ref[sl, :], w_ref[...], preferred_element_type=jnp.float32)
        h_out[sl, :] = h
        xn = h * lax.rsqrt(jnp.mean(h * h, axis=-1, keepdims=True) + EPS) * g_ref[...]
        xp_out[sl, :] = pltpu.pack_elementwise([xn[:, :half], xn[:, half:]], packed_dtype=jnp.bfloat16)
        x_hi = xn.astype(jnp.bfloat16)
        x_lo = (xn - x_hi.astype(jnp.float32)).astype(jnp.bfloat16)
        logits = (jnp.dot(x_hi, wr_hi_ref[...], preferred_element_type=jnp.float32)
                  + jnp.dot(x_lo, wr_hi_ref[...], preferred_element_type=jnp.float32)
                  + jnp.dot(x_hi, wr_lo_ref[...], preferred_element_type=jnp.float32))
        logits = jnp.where(lane < N_EXPERTS, logits, NEG_INF)
        m = jnp.max(logits, axis=-1, keepdims=True)
        e = jnp.exp(logits - m)
        probs_out[sl, :] = e / jnp.sum(e, axis=-1, keepdims=True)


def _out_proj(mixed, w_out, h, g_ffn, wr_hi, wr_lo, layer, tm=512):
    t = h.shape[0]
    row = lambda w: pl.BlockSpec((tm, w), lambda i: (i, 0))
    lay = lambda a, b: pl.BlockSpec((None, a, b), lambda i: (layer, 0, 0))
    return pl.pallas_call(
        _out_proj_kernel,
        grid=(t // tm,),
        in_specs=[row(D_MODEL), lay(D_MODEL, D_MODEL), row(D_MODEL), lay(1, D_MODEL),
                  lay(D_MODEL, LANES), lay(D_MODEL, LANES)],
        out_specs=[row(D_MODEL), row(D_MODEL // 2), row(LANES)],
        out_shape=[jax.ShapeDtypeStruct((t, D_MODEL), jnp.float32),
                   jax.ShapeDtypeStruct((t, D_MODEL // 2), jnp.uint32),
                   jax.ShapeDtypeStruct((t, LANES), jnp.float32)],
        compiler_params=_cparams(1),
        name="out_proj_router",
    )(mixed, w_out, h, g_ffn, wr_hi, wr_lo)


def _routing_kernel(p_ref, idx_out, gate_out, tri_ref):
    rows, seq = p_ref.shape
    p = p_ref[...]
    tok = lax.broadcasted_iota(jnp.int32, (rows, seq), 1)

    def count(mask):
        return jnp.sum(jnp.where(mask, 1.0, 0.0), axis=-1, keepdims=True)

    def thr_body(i, thr):
        cand = thr | jnp.left_shift(jnp.int32(1), 30 - i)
        cand_f = lax.bitcast_convert_type(cand, jnp.float32)
        return jnp.where(count(p >= cand_f) >= CAPACITY, cand, thr)

    thr = lax.bitcast_convert_type(lax.fori_loop(0, 31, thr_body, jnp.zeros((rows, 1), jnp.int32)), jnp.float32)
    gt = p > thr
    eq = p == thr
    need = CAPACITY - count(gt)

    def tie_body(i, bound):
        cand = bound | jnp.left_shift(jnp.int32(1), 11 - i)
        return jnp.where(count(jnp.logical_and(eq, tok < cand)) < need, cand, bound)

    bound = lax.fori_loop(0, 12, tie_body, jnp.zeros((rows, 1), jnp.int32))
    sel = jnp.logical_or(gt, jnp.logical_and(eq, tok <= bound))

    r_i = lax.broadcasted_iota(jnp.int32, (seq, seq), 0)
    c_i = lax.broadcasted_iota(jnp.int32, (seq, seq), 1)
    tri_ref[...] = jnp.where(r_i < c_i, 1.0, 0.0).astype(jnp.bfloat16)
    self_bf = jnp.where(sel, 1.0, 0.0).astype(jnp.bfloat16)
    pos = jnp.dot(self_bf, tri_ref[...], preferred_element_type=jnp.float32)
    pos = jnp.where(sel, pos, -1.0)

    bf = jnp.bfloat16
    p1 = p.astype(bf).astype(jnp.float32)
    r1 = p - p1
    p2 = r1.astype(bf).astype(jnp.float32)
    p3 = (r1 - p2).astype(bf).astype(jnp.float32)
    t_hi = (tok[0:1] >> 6).astype(jnp.float32)
    t_lo = (tok[0:1] & 63).astype(jnp.float32)
    slot = lax.broadcasted_iota(jnp.int32, (CAPACITY, seq), 0).astype(jnp.float32)
    prow = lax.broadcasted_iota(jnp.int32, (8, seq), 0)
    for r in range(rows):
        onehot = jnp.where(pos[r:r + 1, :] == slot, 1.0, 0.0).astype(bf)
        payload = jnp.where(prow == 0, t_hi, jnp.where(prow == 1, t_lo, jnp.where(
            prow == 2, p1[r:r + 1], jnp.where(prow == 3, p2[r:r + 1], jnp.where(prow == 4, p3[r:r + 1], 0.0)))))
        g = lax.dot_general(payload.astype(bf), onehot, (((1,), (1,)), ((), ())),
                            preferred_element_type=jnp.float32)
        idx_out[r:r + 1, :] = (g[0:1] * 64.0 + g[1:2]).astype(jnp.int32)
        gate_out[r:r + 1, :] = g[2:3] + g[3:4] + g[4:5]


def _routing(probs_t):
    rows, seq = probs_t.shape
    rt = N_EXPERTS
    return pl.pallas_call(
        _routing_kernel,
        grid=(rows // rt,),
        in_specs=[pl.BlockSpec((rt, seq), lambda i: (i, 0))],
        out_specs=[pl.BlockSpec((rt, CAPACITY), lambda i: (i, 0)), pl.BlockSpec((rt, CAPACITY), lambda i: (i, 0))],
        out_shape=[jax.ShapeDtypeStruct((rows, CAPACITY), jnp.int32),
                   jax.ShapeDtypeStruct((rows, CAPACITY), jnp.float32)],
        scratch_shapes=[pltpu.VMEM((seq, seq), jnp.bfloat16)],
        compiler_params=_cparams(1),
        name="expert_choice_routing",
    )(probs_t)


def _gather_kernel(idx_ref, x_ref, o_ref):
    def body(i, c):
        t = idx_ref[0, i]
        o_ref[pl.ds(i, 1), :] = x_ref[pl.ds(t, 1), :]
        return c
    lax.fori_loop(0, CAPACITY, body, 0, unroll=8)


def _gather(idx, x_packed):
    b, e = idx.shape[:2]
    w = x_packed.shape[-1]
    return pl.pallas_call(
        _gather_kernel,
        grid=(b, e),
        in_specs=[pl.BlockSpec((None, None, 1, CAPACITY), lambda i, j: (i, j, 0, 0), memory_space=pltpu.SMEM),
                  pl.BlockSpec((None, SEQ, w), lambda i, j: (i, 0, 0))],
        out_specs=pl.BlockSpec((None, None, CAPACITY, w), lambda i, j: (i, j, 0, 0)),
        out_shape=jax.ShapeDtypeStruct((b, e, CAPACITY, w), jnp.uint32),
        compiler_params=_cparams(2),
        name="moe_gather",
    )(idx, x_packed)


FFN_CHUNKS = 4


def _ffn_kernel(x_ref, gate_ref, wg_ref, wu_ref, wd_ref, o_ref, xs_ref, hs_ref):
    bf = jnp.bfloat16
    f = pl.program_id(2)
    rows = xs_ref.shape[0]

    @pl.when(f == 0)
    def _():
        xp = x_ref[...].reshape(rows, x_ref.shape[2])
        lo = pltpu.unpack_elementwise(xp, index=0, packed_dtype=bf, unpacked_dtype=jnp.float32).astype(bf)
        hi = pltpu.unpack_elementwise(xp, index=1, packed_dtype=bf, unpacked_dtype=jnp.float32).astype(bf)
        xs_ref[...] = jnp.concatenate([lo, hi], axis=1)

    x = xs_ref[...]
    a = jnp.dot(x, wg_ref[...].astype(bf), preferred_element_type=jnp.float32)
    u = jnp.dot(x, wu_ref[...].astype(bf), preferred_element_type=jnp.float32)
    hs_ref[f] = (a / (1.0 + jnp.exp(-a)) * u).astype(bf)

    @pl.when(f == FFN_CHUNKS - 1)
    def _():
        h_all = jnp.concatenate([hs_ref[c] for c in range(FFN_CHUNKS)], axis=1)
        y = jnp.dot(h_all, wd_ref[...].astype(bf), preferred_element_type=jnp.float32)
        out = y * gate_ref[...].reshape(rows, 1)
        o_ref[...] = out.reshape(o_ref.shape).astype(o_ref.dtype)


def _ffn(xe, gate, w_gate, w_up, w_down, layer):
    b, e, c, w = xe.shape
    bb = 4 if b % 4 == 0 else 1
    fc = EXPERT_FF // FFN_CHUNKS
    return pl.pallas_call(
        _ffn_kernel,
        grid=(e, b // bb, FFN_CHUNKS),
        in_specs=[pl.BlockSpec((bb, None, c, w), lambda j, i, f: (i, j, 0, 0)),
                  pl.BlockSpec((bb, None, c, 1), lambda j, i, f: (i, j, 0, 0)),
                  pl.BlockSpec((None, None, D_MODEL, fc), lambda j, i, f: (layer, j, 0, f)),
                  pl.BlockSpec((None, None, D_MODEL, fc), lambda j, i, f: (layer, j, 0, f)),
                  pl.BlockSpec((None, None, EXPERT_FF, D_MODEL), lambda j, i, f: (layer, j, 0, 0))],
        out_specs=pl.BlockSpec((bb, None, c, D_MODEL), lambda j, i, f: (i, j, 0, 0)),
        out_shape=jax.ShapeDtypeStruct((b, e, c, D_MODEL), jnp.bfloat16),
        scratch_shapes=[pltpu.VMEM((bb * c, D_MODEL), jnp.bfloat16),
                        pltpu.VMEM((FFN_CHUNKS, bb * c, fc), jnp.bfloat16)],
        compiler_params=_cparams(3),
        name="expert_ffn",
    )(xe, gate, w_gate, w_up, w_down)


def _combine_kernel(idx_ref, y_ref, h_ref, g_ref, *out_refs, final):
    tm = h_ref.shape[0]
    tok = pl.program_id(1) * tm + lax.broadcasted_iota(jnp.int32, (tm, idx_ref.shape[1]), 0)
    sel = jnp.where(tok == idx_ref[...], 1.0, 0.0).astype(y_ref.dtype)
    h = h_ref[...] + jnp.dot(sel, y_ref[...], preferred_element_type=jnp.float32)
    xn = h * lax.rsqrt(jnp.mean(h * h, axis=-1, keepdims=True) + EPS) * g_ref[...]
    if final:
        out_refs[0][...] = xn.astype(out_refs[0].dtype)
    else:
        out_refs[0][...] = h
        out_refs[1][...] = xn.astype(out_refs[1].dtype)


def _combine(idx, ye, h, g_next, final, tm=256):
    b, n_slots = idx.shape[0], idx.shape[-1]
    row = pl.BlockSpec((None, tm, D_MODEL), lambda i, r: (i, r, 0))
    shape = lambda dt: jax.ShapeDtypeStruct((b, SEQ, D_MODEL), dt)
    return pl.pallas_call(
        functools.partial(_combine_kernel, final=final),
        grid=(b, SEQ // tm),
        in_specs=[pl.BlockSpec((None, 1, n_slots), lambda i, r: (i, 0, 0)),
                  pl.BlockSpec((None, n_slots, D_MODEL), lambda i, r: (i, 0, 0)),
                  row,
                  pl.BlockSpec((1, D_MODEL), lambda i, r: (0, 0))],
        out_specs=[row] if final else [row, row],
        out_shape=[shape(jnp.float32)] if final else [shape(jnp.float32), shape(jnp.bfloat16)],
        compiler_params=_cparams(2),
        name="moe_combine",
    )(idx, ye, h, g_next.reshape(1, D_MODEL))


def _alibi_slopes(n):
    return jnp.exp2(-8.0 * jnp.arange(1, n + 1, dtype=jnp.float32) / n)


def _token_mixer(xn, l, p, tabs):
    t = xn.shape[0]
    b = t // SEQ
    bf = jnp.bfloat16
    groups = _proj_dilated(xn, p["w_all"], l)
    proj = _proj_rest(xn, p["w_all"], l, tm=min(t, 2048))
    proj4 = proj.reshape(b, 1, SEQ, REST_COLS)

    dils = jnp.asarray([dil for _, dil in A_DILATIONS], jnp.float32)
    o_a = _dilated_attention(groups, _alibi_slopes(A_HEADS)[None, :] * dils[:, None])

    qb, kb, vb = _mla_prep(proj, p["g_mla_q"][l], p["g_mla_kv"][l], p["w_mla_uq"][l], p["w_mla_ukv"][l],
                           tabs["mla"], (MLA_NOPE + MLA_ROPE) ** -0.5)
    o_b = _dense_attention(qb, kb, vb,
                           pl.BlockSpec((SEQ, 2 * LANES), lambda i, j: (i, j)),
                           pl.BlockSpec((SEQ, 2 * LANES), lambda i, j: (i, j)),
                           pl.BlockSpec((SEQ, 2 * LANES), lambda i, j: (i, j)), True, "latent_attention")

    qc, kc, vc = _c_prep(proj, p["g_c_q"][l], p["g_c_k"][l], tabs["axial"], HEAD_DIM ** -0.5)
    o_c = _dense_attention(qc, kc, vc,
                           pl.BlockSpec((SEQ, LANES), lambda i, j: (i, j)),
                           pl.BlockSpec((SEQ, LANES), lambda i, j: (i, j // 2)),
                           pl.BlockSpec((SEQ, LANES), lambda i, j: (i, j // 2)), False, "axial_attention")

    o_d = _sink_attention(proj4, _alibi_slopes(D_Q_HEADS), p["sink_d"][l].astype(jnp.float32))

    return _merge(xn, (o_a, o_b, o_c, o_d), p["w_all"], p["b_gate"], p["w_branch"], l)


def _layer(h, xn, l, p, tabs, g_next, final):
    t = h.shape[0]
    b = t // SEQ
    mixed = _token_mixer(xn, l, p, tabs)
    h, x_packed, probs = _out_proj(mixed, p["w_out"], h, p["g_ffn_norm"], p["wr_hi"], p["wr_lo"], l)
    probs_t = jnp.transpose(probs.reshape(b, SEQ, LANES)[:, :, :N_EXPERTS], (0, 2, 1)).reshape(b * N_EXPERTS, SEQ)
    idx, gate = _routing(probs_t)
    xe = _gather(idx.reshape(b, N_EXPERTS, 1, CAPACITY), x_packed.reshape(b, SEQ, D_MODEL // 2))
    ye = _ffn(xe, gate.reshape(b, N_EXPERTS, CAPACITY, 1), p["w_exp_gate"], p["w_exp_up"], p["w_exp_down"], l)
    outs = _combine(idx.reshape(b, 1, N_EXPERTS * CAPACITY), ye.reshape(b, N_EXPERTS * CAPACITY, D_MODEL),
                    h.reshape(b, SEQ, D_MODEL), g_next, final)
    return [o.reshape(t, D_MODEL) for o in outs]


def _prepare_params(w_in, w_mla_uq, w_mla_ukv, w_branch, w_out, w_router, w_exp_gate, w_exp_up, w_exp_down):
    bf = jnp.bfloat16
    p = {}
    p["w_all"] = _relayout_w_in(w_in)
    p["w_mla_uq"] = jnp.stack([_mla_uq_layout(w_mla_uq[l]) for l in range(DEPTH)]).astype(bf)
    p["w_mla_ukv"] = jnp.stack([_mla_ukv_layout(w_mla_ukv[l]) for l in range(DEPTH)]).astype(bf)
    p["w_branch"] = w_branch.astype(bf)
    p["w_out"] = w_out.astype(bf)
    wr = jnp.pad(w_router, ((0, 0), (0, 0), (0, LANES - N_EXPERTS)))
    p["wr_hi"] = wr.astype(bf)
    p["wr_lo"] = (wr - p["wr_hi"].astype(jnp.float32)).astype(bf)
    p["w_exp_gate"] = w_exp_gate
    p["w_exp_up"] = w_exp_up
    p["w_exp_down"] = w_exp_down
    return p


def kernel(x, w_in, b_gate, g_attn_norm, g_ffn_norm, g_mla_q, g_mla_kv, w_mla_uq, w_mla_ukv, g_c_q, g_c_k, sink_d,
           w_branch, w_out, w_router, w_exp_gate, w_exp_up, w_exp_down, g_final):
    b, s, d = x.shape
    assert (s, d) == (SEQ, D_MODEL)
    p = _prepare_params(w_in, w_mla_uq, w_mla_ukv, w_branch, w_out, w_router, w_exp_gate, w_exp_up, w_exp_down)
    p.update(b_gate=b_gate, g_attn_norm=g_attn_norm, g_ffn_norm=g_ffn_norm.reshape(DEPTH, 1, D_MODEL),
             g_mla_q=g_mla_q, g_mla_kv=g_mla_kv, g_c_q=g_c_q, g_c_k=g_c_k, sink_d=sink_d)
    pos = np.arange(SEQ)
    rows, cols = pos // GRID_W, pos % GRID_W
    tabs = {
        "mla": _rope_tables([pos], MLA_ROPE, [MLA_NOPE]),
        "axial": _rope_tables([rows, cols, rows, cols], HEAD_DIM // 2, [0, 32, 64, 96]),
    }
    h = x.reshape(b * s, d)
    xn = _rmsnorm(h, g_attn_norm[0], jnp.bfloat16)
    for l in range(DEPTH - 1):
        h, xn = _layer(h, xn, l, p, tabs, g_attn_norm[l + 1], False)
    (out,) = _layer(h, xn, DEPTH - 1, p, tabs, g_final, True)
    return out.reshape(b, s, d)
```

```python
import functools

import numpy as np
import jax
import jax.numpy as jnp
from jax import lax
from jax.experimental import pallas as pl
from jax.experimental.pallas import tpu as pltpu

D_MODEL = 2048
SEQ = 2048
DEPTH = 2
HEAD_DIM = 64
GRID_W = 64
ROPE_THETA = 10000.0
NEG_INF = -1e30
EPS = 1e-6
N_BRANCH = 4
BRANCH_WIDTH = 512

A_HEADS = 8
A_DILATIONS = ((128, 1), (512, 4), (2048, 16))
N_DIL = 3
MLA_HEADS = 8
MLA_Q_LORA = 512
MLA_KV_LORA = 256
MLA_NOPE = 64
MLA_ROPE = 32
MLA_V = 64
C_Q_HEADS = 8
C_KV_HEADS = 2
D_Q_HEADS = 8
D_KV_HEADS = 2
D_HALF_WINDOW = 128

A_COLS = 3 * N_DIL * A_HEADS * HEAD_DIM
B_COLS = MLA_Q_LORA + MLA_KV_LORA + MLA_ROPE
C_COLS = (C_Q_HEADS + 2 * C_KV_HEADS) * HEAD_DIM
D_COLS = (D_Q_HEADS + 2 * D_KV_HEADS) * HEAD_DIM
GATE_COL0 = A_COLS + B_COLS + C_COLS + D_COLS

N_EXPERTS = 16
EXPERT_FF = D_MODEL // 2
CAPACITY = 2 * SEQ // N_EXPERTS

LANES = 128
N_PAIRS = 4
DEN_LANE = HEAD_DIM
VMEM_LIMIT = 56 * 1024 * 1024

GROUP_COLS = 3 * A_HEADS * HEAD_DIM
REST_COLS = 3072
GATE_COLS = N_BRANCH * D_MODEL
BLK_CQ = 0
BLK_QC = 4
BLK_QD = 8
BLK_CKV = 12
BLK_KPE = 14
BLK_KC = 16
BLK_VC = 18
BLK_KD = 20
BLK_VD = 22


def _cparams(n_grid_dims):
    return pltpu.CompilerParams(dimension_semantics=("arbitrary",) * n_grid_dims, vmem_limit_bytes=VMEM_LIMIT)


def _proj_segments():
    segs = [(0, A_COLS)]
    b0 = A_COLS
    c0 = A_COLS + B_COLS
    d0 = c0 + C_COLS
    segs.append((b0, MLA_Q_LORA))
    segs.append((c0, C_Q_HEADS * HEAD_DIM))
    segs.append((d0, D_Q_HEADS * HEAD_DIM))
    segs.append((b0 + MLA_Q_LORA, MLA_KV_LORA))
    segs += [(None, 64), (b0 + MLA_Q_LORA + MLA_KV_LORA, MLA_ROPE), (None, 32 + LANES)]
    for base, nq, nkv in ((c0, C_Q_HEADS, C_KV_HEADS), (d0, D_Q_HEADS, D_KV_HEADS)):
        for part in range(2):
            for kv in range(nkv):
                start = base + (nq + part * nkv + kv) * HEAD_DIM
                segs += [(start, HEAD_DIM), (start, HEAD_DIM)]
    assert sum(w for _, w in segs) == N_DIL * GROUP_COLS + REST_COLS
    segs.append((GATE_COL0, GATE_COLS))
    return segs


def _relayout_w_in(w_in):
    wt = jnp.transpose(w_in, (0, 2, 1))
    parts = []
    for start, width in _proj_segments():
        if start is None:
            parts.append(jnp.zeros((wt.shape[0], width, wt.shape[2]), jnp.bfloat16))
        else:
            parts.append(wt[:, start:start + width, :].astype(jnp.bfloat16))
    return jnp.concatenate(parts, axis=1)


def _dot_nt(x, w):
    return lax.dot_general(x, w, (((1,), (1,)), ((), ())), preferred_element_type=jnp.float32)


def _mla_uq_layout(w_uq):
    w = w_uq.reshape(MLA_Q_LORA, MLA_HEADS, MLA_NOPE + MLA_ROPE)
    w = jnp.pad(w, ((0, 0), (0, 0), (0, LANES - MLA_NOPE - MLA_ROPE)))
    return w.reshape(MLA_Q_LORA, MLA_HEADS * LANES)


def _mla_ukv_layout(w_ukv):
    w = w_ukv.reshape(MLA_KV_LORA, MLA_HEADS, MLA_NOPE + MLA_V)
    wk = jnp.pad(w[:, :, :MLA_NOPE], ((0, 0), (0, 0), (0, LANES - MLA_NOPE)))
    wv = jnp.pad(w[:, :, MLA_NOPE:], ((0, 0), (0, 0), (0, LANES - MLA_V)))
    return jnp.concatenate([wk.reshape(MLA_KV_LORA, MLA_HEADS * LANES),
                            wv.reshape(MLA_KV_LORA, MLA_HEADS * LANES)], axis=1)


def _rope_tables(pos_list, dim, lane_offsets):
    half = dim // 2
    freqs = ROPE_THETA ** (-np.arange(0, dim, 2, dtype=np.float64) / dim)
    cos_t = np.ones((SEQ, LANES), np.float64)
    sin_a = np.zeros((SEQ, LANES), np.float64)
    sin_b = np.zeros((SEQ, LANES), np.float64)
    for pos, off in zip(pos_list, lane_offsets):
        ang = np.asarray(pos, np.float64)[:, None] * freqs[None, :]
        c, s = np.cos(ang), np.sin(ang)
        cos_t[:, off:off + half] = c
        cos_t[:, off + half:off + dim] = c
        sin_a[:, off:off + half] = -s
        sin_b[:, off + half:off + dim] = s
    return tuple(jnp.asarray(t, jnp.float32) for t in (cos_t, sin_a, sin_b))


def _apply_rope(x, cos_t, sin_a, sin_b, half):
    n = x.shape[-1]
    return (x * cos_t + pltpu.roll(x, n - half, 1) * sin_a + pltpu.roll(x, half, 1) * sin_b)


def _rmsnorm_kernel(x_ref, g_ref, o_ref):
    x = x_ref[...]
    ms = jnp.mean(x * x, axis=-1, keepdims=True)
    o_ref[...] = (x * lax.rsqrt(ms + EPS) * g_ref[...]).astype(o_ref.dtype)


def _rmsnorm(x, g, out_dtype, tm=512):
    t, d = x.shape
    return pl.pallas_call(
        _rmsnorm_kernel,
        grid=(t // tm,),
        in_specs=[pl.BlockSpec((tm, d), lambda i: (i, 0)), pl.BlockSpec((1, d), lambda i: (0, 0))],
        out_specs=pl.BlockSpec((tm, d), lambda i: (i, 0)),
        out_shape=jax.ShapeDtypeStruct((t, d), out_dtype),
        compiler_params=_cparams(1),
        name="rmsnorm",
    )(x, g.reshape(1, d))


def _matmul_kernel(x_ref, w_ref, o_ref):
    o_ref[...] = _dot_nt(x_ref[...], w_ref[...]).astype(o_ref.dtype)


def _proj_rest(x, w_all, layer, tm, tn=1536):
    m, k = x.shape
    blk0 = N_DIL * GROUP_COLS // tn
    return pl.pallas_call(
        _matmul_kernel,
        grid=(m // tm, REST_COLS // tn),
        in_specs=[pl.BlockSpec((tm, k), lambda i, j: (i, 0)),
                  pl.BlockSpec((None, tn, k), lambda i, j: (layer, blk0 + j, 0))],
        out_specs=pl.BlockSpec((tm, tn), lambda i, j: (i, j)),
        out_shape=jax.ShapeDtypeStruct((m, REST_COLS), jnp.bfloat16),
        compiler_params=_cparams(2),
        name="in_proj",
    )(x, w_all)


def _proj_dilated_kernel(x_ref, w0_ref, w1_ref, w2_ref, o0_ref, o1_ref, o2_ref, scr_ref):
    x = x_ref[...]
    for w_ref, o_ref, (_, dil) in zip((w0_ref, w1_ref, w2_ref), (o0_ref, o1_ref, o2_ref), A_DILATIONS):
        acc = _dot_nt(x, w_ref[...])
        if dil == 1:
            o_ref[0] = acc.astype(o_ref.dtype)
        else:
            n_blk = acc.shape[1] // LANES
            for c in range(n_blk):
                scr_ref[c] = acc[:, c * LANES:(c + 1) * LANES]
            for r in range(dil):
                for c in range(n_blk):
                    rows = scr_ref[c, pl.ds(r, SEQ // dil, stride=dil), :]
                    o_ref[r, :, c * LANES:(c + 1) * LANES] = rows.astype(o_ref.dtype)


def _proj_dilated(x, w_all, layer, tn=512):
    t, k = x.shape
    b = t // SEQ
    nc = GROUP_COLS // tn

    def w_spec(g):
        return pl.BlockSpec((None, tn, k), lambda i, c, g=g: (layer, c * N_DIL + g, 0))

    out_specs, out_shape = [], []
    for _, dil in A_DILATIONS:
        ld = SEQ // dil
        out_specs.append(pl.BlockSpec((None, dil, ld, tn), lambda i, c: (i, 0, 0, c)))
        out_shape.append(jax.ShapeDtypeStruct((b, dil, ld, GROUP_COLS), jnp.bfloat16))
    return pl.pallas_call(
        _proj_dilated_kernel,
        grid=(b, nc),
        in_specs=[pl.BlockSpec((SEQ, k), lambda i, c: (i, 0)), w_spec(0), w_spec(1), w_spec(2)],
        out_specs=out_specs,
        out_shape=out_shape,
        scratch_shapes=[pltpu.VMEM((tn // LANES, SEQ, LANES), jnp.float32)],
        compiler_params=_cparams(2),
        name="in_proj_dilated",
    )(x, w_all, w_all, w_all)


BANDED_GROUP_SCORE_ELEMS = 256 * 1024


def _banded_tiles(half_window, length):
    return 2 * half_window, min(4 * half_window, length)


def _banded_sequences(slopes, sinks, q_ref, k_ref, v_ref, bias_ref, write, *, length, half_window, dil):
    use_sink = sinks is not None
    w = half_window
    tq, tk = _banded_tiles(w, length)
    nq = length // tq
    key_start = [min(max(qb * tq - w, 0), length - tk) for qb in range(nq)]
    offsets = sorted({key_start[qb] - qb * tq for qb in range(nq)})

    lane = lax.broadcasted_iota(jnp.int32, (1, LANES), 1)
    half0 = lane < HEAD_DIM
    delta = lax.broadcasted_iota(jnp.int32, (tq, tk), 1) - lax.broadcasted_iota(jnp.int32, (tq, tk), 0)
    for oi, off in enumerate(offsets):
        rel = jnp.abs(delta + off)
        dist = rel.astype(jnp.float32)
        for i in range(2):
            bias_ref[2 * oi + i] = jnp.where(rel <= w, -slopes[i] * dist, NEG_INF)

    scale = jnp.asarray(HEAD_DIM ** -0.5, q_ref.dtype)
    blocks = [(r, qb) for r in range(dil) for qb in range(nq)]
    group = max(2, BANDED_GROUP_SCORE_ELEMS // (2 * tq * tk))
    for g0 in range(0, len(blocks), group):
        grp = blocks[g0:g0 + group]
        scores = []
        for r, qb in grp:
            qs, ks = qb * tq, key_start[qb]
            oi = offsets.index(ks - qs)
            q = q_ref[r, qs:qs + tq, :] * scale
            kw = k_ref[r, ks:ks + tk, :]
            for i in range(2):
                qi = jnp.where(half0 if i == 0 else jnp.logical_not(half0), q, jnp.zeros_like(q))
                scores.append(_dot_nt(qi, kw) + bias_ref[2 * oi + i])
        probs, maxes, dens = [], [], []
        for n, s in enumerate(scores):
            m = jnp.max(s, axis=-1, keepdims=True)
            if use_sink:
                sk = sinks[n % 2]
                m = jnp.maximum(m, sk)
            p = jnp.exp(s - m)
            den = jnp.sum(p, axis=-1, keepdims=True)
            if use_sink:
                den = den + jnp.exp(sk - m)
            probs.append(p.astype(v_ref.dtype))
            maxes.append(m)
            dens.append(den)
        for n, (r, qb) in enumerate(grp):
            qs, ks = qb * tq, key_start[qb]
            vw = v_ref[r, ks:ks + tk, :]
            outs = [jnp.dot(probs[2 * n + i], vw, preferred_element_type=jnp.float32) / dens[2 * n + i]
                    for i in range(2)]
            rows = pl.ds(qs, tq) if dil == 1 else pl.ds(r + qs * dil, tq, stride=dil)
            lses = [maxes[2 * n + i] + jnp.log(dens[2 * n + i]) for i in range(2)]
            write(rows, jnp.where(half0, outs[0], outs[1]), jnp.where(half0, lses[0], lses[1]))


def _bias_scratch(half_window, length):
    tq, tk = _banded_tiles(half_window, length)
    return pltpu.VMEM((2 if tk == length else 6, tq, tk), jnp.float32)


def _sink_kernel(slope_ref, sink_ref, q_ref, k_ref, v_ref, o_ref, bias_ref):
    hp = pl.program_id(1)

    def write(rows, o, lse):
        del lse
        o_ref[rows, :] = o.astype(o_ref.dtype)

    _banded_sequences([slope_ref[2 * hp + i] for i in range(2)], [sink_ref[2 * hp + i] for i in range(2)],
                      q_ref, k_ref, v_ref, bias_ref, write, length=SEQ, half_window=D_HALF_WINDOW, dil=1)


def _sink_attention(proj4, slopes, sinks):
    b = proj4.shape[0]
    smem = pl.BlockSpec(memory_space=pltpu.SMEM)

    def spec(blk_fn):
        return pl.BlockSpec((None, 1, SEQ, LANES), lambda i, j: (i, 0, 0, blk_fn(j)))

    return pl.pallas_call(
        _sink_kernel,
        grid=(b, N_PAIRS),
        in_specs=[smem, smem, spec(lambda j: BLK_QD + j), spec(lambda j: BLK_KD + j // 2),
                  spec(lambda j: BLK_VD + j // 2)],
        out_specs=pl.BlockSpec((SEQ, LANES), lambda i, j: (i, j)),
        out_shape=jax.ShapeDtypeStruct((b * SEQ, N_PAIRS * LANES), jnp.bfloat16),
        scratch_shapes=[_bias_scratch(D_HALF_WINDOW, SEQ)],
        compiler_params=_cparams(2),
        name="banded_sink",
    )(slopes, sinks, proj4, proj4, proj4)


def _dilated_kernel(slope_ref, *refs):
    qkv_refs = refs[:9]
    o_ref = refs[9]
    bias_refs = refs[10:13]
    o_stage, lse_stage = refs[13], refs[14]
    hp = pl.program_id(1)
    for g, (window, dil) in enumerate(A_DILATIONS):
        def write(rows, o, lse, g=g):
            o_stage[g, rows, :] = o
            lse_stage[g, rows, :] = lse

        _banded_sequences([slope_ref[g, 2 * hp + i] for i in range(2)], None, *qkv_refs[3 * g:3 * g + 3],
                          bias_refs[g], write, length=SEQ // dil, half_window=(window // 2) // dil, dil=dil)
    la, lb, lc = lse_stage[0], lse_stage[1], lse_stage[2]
    m = jnp.maximum(jnp.maximum(la, lb), lc)
    ea, eb, ec = jnp.exp(la - m), jnp.exp(lb - m), jnp.exp(lc - m)
    num = ea * o_stage[0] + eb * o_stage[1] + ec * o_stage[2]
    o_ref[...] = (num / (ea + eb + ec)).astype(o_ref.dtype)


def _dilated_attention(groups, slopes):
    b = groups[0].shape[0]
    in_specs = [pl.BlockSpec(memory_space=pltpu.SMEM)]
    operands = [slopes]
    scratch = []
    for g, (window, dil) in enumerate(A_DILATIONS):
        ld = SEQ // dil
        for s in range(3):
            in_specs.append(pl.BlockSpec((None, dil, ld, LANES), lambda i, j, s=s: (i, 0, 0, s * N_PAIRS + j)))
            operands.append(groups[g])
        scratch.append(_bias_scratch((window // 2) // dil, ld))
    scratch += [pltpu.VMEM((N_DIL, SEQ, LANES), jnp.float32)] * 2
    return pl.pallas_call(
        _dilated_kernel,
        grid=(b, N_PAIRS),
        in_specs=in_specs,
        out_specs=pl.BlockSpec((SEQ, LANES), lambda i, j: (i, j)),
        out_shape=jax.ShapeDtypeStruct((b * SEQ, N_PAIRS * LANES), jnp.bfloat16),
        scratch_shapes=scratch,
        compiler_params=_cparams(2),
        name="banded_dilated",
    )(*operands)


def _mla_prep_kernel(cq_ref, ckv_ref, kpe_ref, gq_ref, gkv_ref, wuq_ref, wukv_ref, cos_ref, sa_ref, sb_ref,
                     q_out, k_out, v_out, *, scale):
    def norm(x, g):
        x = x.astype(jnp.float32)
        return (x * lax.rsqrt(jnp.mean(x * x, axis=-1, keepdims=True) + EPS) * g).astype(jnp.bfloat16)

    cos_t, sin_a, sin_b = cos_ref[...], sa_ref[...], sb_ref[...]
    half = MLA_ROPE // 2
    q = jnp.dot(norm(cq_ref[...], gq_ref[...]), wuq_ref[...], preferred_element_type=jnp.float32)
    kv = jnp.dot(norm(ckv_ref[...], gkv_ref[...]), wukv_ref[...], preferred_element_type=jnp.float32)
    kpe = _apply_rope(kpe_ref[...].astype(jnp.float32), cos_t, sin_a, sin_b, half)
    for h in range(MLA_HEADS):
        sl = slice(h * LANES, (h + 1) * LANES)
        q_out[:, sl] = (_apply_rope(q[:, sl], cos_t, sin_a, sin_b, half) * scale).astype(q_out.dtype)
        k_out[:, sl] = (kv[:, sl] + kpe).astype(k_out.dtype)
    lane = lax.broadcasted_iota(jnp.int32, (1, MLA_HEADS * LANES), 1)
    ones_lane = jnp.where(lane % LANES == DEN_LANE, 1.0, 0.0)
    v_out[...] = (kv[:, MLA_HEADS * LANES:] + ones_lane).astype(v_out.dtype)


def _mla_prep(proj, g_q, g_kv, w_uq, w_ukv, tables, scale, tm=512):
    t = proj.shape[0]
    nblk = SEQ // tm
    tab_spec = pl.BlockSpec((tm, LANES), lambda i: (i % nblk, 0))
    nq, nk, nv = MLA_HEADS * LANES, MLA_HEADS * LANES, MLA_HEADS * LANES
    return pl.pallas_call(
        functools.partial(_mla_prep_kernel, scale=scale),
        grid=(t // tm,),
        in_specs=[pl.BlockSpec((tm, MLA_Q_LORA), lambda i: (i, BLK_CQ * LANES // MLA_Q_LORA)),
                  pl.BlockSpec((tm, MLA_KV_LORA), lambda i: (i, BLK_CKV * LANES // MLA_KV_LORA)),
                  pl.BlockSpec((tm, LANES), lambda i: (i, BLK_KPE)),
                  pl.BlockSpec((1, MLA_Q_LORA), lambda i: (0, 0)),
                  pl.BlockSpec((1, MLA_KV_LORA), lambda i: (0, 0)),
                  pl.BlockSpec(w_uq.shape, lambda i: (0, 0)),
                  pl.BlockSpec(w_ukv.shape, lambda i: (0, 0)),
                  tab_spec, tab_spec, tab_spec],
        out_specs=[pl.BlockSpec((tm, nq), lambda i: (i, 0)),
                   pl.BlockSpec((tm, nk), lambda i: (i, 0)),
                   pl.BlockSpec((tm, nv), lambda i: (i, 0))],
        out_shape=[jax.ShapeDtypeStruct((t, nq), jnp.bfloat16),
                   jax.ShapeDtypeStruct((t, nk), jnp.bfloat16),
                   jax.ShapeDtypeStruct((t, nv), jnp.bfloat16)],
        compiler_params=_cparams(1),
        name="mla_prep",
    )(proj, proj, proj, g_q.reshape(1, -1), g_kv.reshape(1, -1), w_uq, w_ukv, *tables)


def _c_prep_kernel(q_ref, k_ref, v_ref, gq_ref, gk_ref, cos_ref, sa_ref, sb_ref, q_out, k_out, v_out, *, scale):
    lane = lax.broadcasted_iota(jnp.int32, (1, LANES), 1)
    half0 = lane < HEAD_DIM
    cos_t, sin_a, sin_b = cos_ref[...], sa_ref[...], sb_ref[...]
    ones_lane = jnp.where(lane == DEN_LANE, 1.0, 0.0).astype(v_out.dtype)
    for p in range(v_ref.shape[1] // LANES):
        sl = slice(p * LANES, (p + 1) * LANES)
        v_out[:, sl] = jnp.where(half0, v_ref[:, sl], ones_lane)

    def prep(x, g):
        x = x.astype(jnp.float32)
        sq = x * x
        s0 = jnp.sum(jnp.where(half0, sq, 0.0), axis=-1, keepdims=True)
        s1 = jnp.sum(jnp.where(half0, 0.0, sq), axis=-1, keepdims=True)
        inv = jnp.where(half0, lax.rsqrt(s0 / HEAD_DIM + EPS), lax.rsqrt(s1 / HEAD_DIM + EPS))
        return _apply_rope(x * inv * g, cos_t, sin_a, sin_b, HEAD_DIM // 4)

    for p in range(q_ref.shape[1] // LANES):
        sl = slice(p * LANES, (p + 1) * LANES)
        q_out[:, sl] = (prep(q_ref[:, sl], gq_ref[...]) * scale).astype(q_out.dtype)
    for p in range(k_ref.shape[1] // LANES):
        sl = slice(p * LANES, (p + 1) * LANES)
        k_out[:, sl] = prep(k_ref[:, sl], gk_ref[...]).astype(k_out.dtype)


def _c_prep(proj, g_q, g_k, tables, scale, tm=512):
    t = proj.shape[0]
    nblk = SEQ // tm
    tab_spec = pl.BlockSpec((tm, LANES), lambda i: (i % nblk, 0))
    qw, kw = N_PAIRS * LANES, C_KV_HEADS * LANES
    g_q2 = jnp.tile(g_q.reshape(1, HEAD_DIM), (1, 2))
    g_k2 = jnp.tile(g_k.reshape(1, HEAD_DIM), (1, 2))
    return pl.pallas_call(
        functools.partial(_c_prep_kernel, scale=scale),
        grid=(t // tm,),
        in_specs=[pl.BlockSpec((tm, qw), lambda i: (i, BLK_QC * LANES // qw)),
                  pl.BlockSpec((tm, kw), lambda i: (i, BLK_KC * LANES // kw)),
                  pl.BlockSpec((tm, kw), lambda i: (i, BLK_VC * LANES // kw)),
                  pl.BlockSpec((1, LANES), lambda i: (0, 0)),
                  pl.BlockSpec((1, LANES), lambda i: (0, 0)),
                  tab_spec, tab_spec, tab_spec],
        out_specs=[pl.BlockSpec((tm, qw), lambda i: (i, 0)), pl.BlockSpec((tm, kw), lambda i: (i, 0)),
                   pl.BlockSpec((tm, kw), lambda i: (i, 0))],
        out_shape=[jax.ShapeDtypeStruct((t, qw), jnp.bfloat16), jax.ShapeDtypeStruct((t, kw), jnp.bfloat16),
                   jax.ShapeDtypeStruct((t, kw), jnp.bfloat16)],
        compiler_params=_cparams(1),
        name="axial_prep",
    )(proj, proj, proj, g_q2, g_k2, *tables)


DENSE_TILES_PER_STEP = 8


def _dense_kernel(q_ref, k_ref, v_ref, o_ref, *, tq, head_blocks):
    lane = lax.broadcasted_iota(jnp.int32, (1, LANES), 1)
    half0 = lane < HEAD_DIM
    seq = q_ref.shape[0]

    def head_operand(ref, i):
        return ref[:, i * LANES:(i + 1) * LANES] if head_blocks else ref[...]

    def body(t, carry):
        starts = [pl.multiple_of((t * DENSE_TILES_PER_STEP + u) * tq, tq) for u in range(DENSE_TILES_PER_STEP)]
        scores = []
        for qs in starts:
            q = q_ref[pl.ds(qs, tq), :]
            for i in range(2):
                if head_blocks:
                    qi = q[:, i * LANES:(i + 1) * LANES]
                else:
                    qi = jnp.where(half0 if i == 0 else jnp.logical_not(half0), q, jnp.zeros_like(q))
                scores.append(_dot_nt(qi, head_operand(k_ref, i)))
        for u, qs in enumerate(starts):
            outs = []
            for i in range(2):
                s = scores[2 * u + i]
                vi = head_operand(v_ref, i)
                m = jnp.max(s, axis=-1, keepdims=True)
                p = jnp.exp((s - m).astype(vi.dtype))
                pv = jnp.dot(p, vi, preferred_element_type=jnp.float32)
                outs.append(pv * (1.0 / pv[:, DEN_LANE:DEN_LANE + 1]))
            o_ref[pl.ds(qs, tq), :] = jnp.where(half0, outs[0], pltpu.roll(outs[1], HEAD_DIM, 1)).astype(o_ref.dtype)
        return carry

    lax.fori_loop(0, seq // (tq * DENSE_TILES_PER_STEP), body, 0)


def _dense_attention(q, k, v, q_spec, k_spec, v_spec, head_blocks, name, tq=256):
    t = q.shape[0]
    b = t // SEQ
    return pl.pallas_call(
        functools.partial(_dense_kernel, tq=tq, head_blocks=head_blocks),
        grid=(b, N_PAIRS),
        in_specs=[q_spec, k_spec, v_spec],
        out_specs=pl.BlockSpec((SEQ, LANES), lambda i, j: (i, j)),
        out_shape=jax.ShapeDtypeStruct((t, N_PAIRS * LANES), jnp.bfloat16),
        compiler_params=_cparams(2),
        name=name,
    )(q, k, v)


def _merge_kernel(xn_ref, oa_ref, ob_ref, oc_ref, od_ref, wga_ref, wgb_ref, wgc_ref, wgd_ref, bg_ref, wb_ref,
                  out_ref):
    xn = xn_ref[...]
    acc = None
    for j, (o_ref, wg_ref) in enumerate(((oa_ref, wga_ref), (ob_ref, wgb_ref), (oc_ref, wgc_ref), (od_ref, wgd_ref))):
        z = _dot_nt(xn, wg_ref[...]) + bg_ref[j:j + 1, :]
        g = 1.0 / (1.0 + jnp.exp(-z))
        u = jnp.dot(o_ref[...], wb_ref[j], preferred_element_type=jnp.float32)
        acc = g * u if acc is None else acc + g * u
    out_ref[...] = acc.astype(out_ref.dtype)


def _merge(xn, o_list, w_all, b_gate, w_branch, layer, tm=1024, tc=512):
    t = xn.shape[0]
    nc = D_MODEL // tc
    gate_blk0 = (N_DIL * GROUP_COLS + REST_COLS) // tc
    o_spec = pl.BlockSpec((tm, BRANCH_WIDTH), lambda c, i: (i, 0))

    def wg_spec(j):
        return pl.BlockSpec((None, tc, D_MODEL), lambda c, i, j=j: (layer, gate_blk0 + j * nc + c, 0))

    return pl.pallas_call(
        _merge_kernel,
        grid=(nc, t // tm),
        in_specs=[pl.BlockSpec((tm, D_MODEL), lambda c, i: (i, 0))] + [o_spec] * 4
        + [wg_spec(0), wg_spec(1), wg_spec(2), wg_spec(3),
           pl.BlockSpec((None, N_BRANCH, tc), lambda c, i: (layer, 0, c)),
           pl.BlockSpec((None, N_BRANCH, BRANCH_WIDTH, tc), lambda c, i: (layer, 0, 0, c))],
        out_specs=pl.BlockSpec((tm, tc), lambda c, i: (i, c)),
        out_shape=jax.ShapeDtypeStruct((t, D_MODEL), jnp.bfloat16),
        compiler_params=_cparams(2),
        name="gated_merge",
    )(xn, *o_list, w_all, w_all, w_all, w_all, b_gate, w_branch)


OUT_PROJ_SUBTILES = 2


def _out_proj_kernel(mixed_ref, w_ref, h_ref, g_ref, wr_hi_ref, wr_lo_ref, h_out, xp_out, probs_out):
    rows = mixed_ref.shape[0] // OUT_PROJ_SUBTILES
    half = D_MODEL // 2
    lane = lax.broadcasted_iota(jnp.int32, (1, LANES), 1)
    for t in range(OUT_PROJ_SUBTILES):
        sl = slice(t * rows, (t + 1) * rows)
        h = h_ref[sl, :] + jnp.dot(mixed_ref[sl, :], w_ref[...], preferred_element_type=jnp.float32)
        h_out[sl, :] = h
        xn = h * lax.rsqrt(jnp.mean(h * h, axis=-1, keepdims=True) + EPS) * g_ref[...]
        xp_out[sl, :] = pltpu.pack_elementwise([xn[:, :half], xn[:, half:]], packed_dtype=jnp.bfloat16)
        x_hi = xn.astype(jnp.bfloat16)
        x_lo = (xn - x_hi.astype(jnp.float32)).astype(jnp.bfloat16)
        logits = (jnp.dot(x_hi, wr_hi_ref[...], preferred_element_type=jnp.float32)
                  + jnp.dot(x_lo, wr_hi_ref[...], preferred_element_type=jnp.float32)
                  + jnp.dot(x_hi, wr_lo_ref[...], preferred_element_type=jnp.float32))
        logits = jnp.where(lane < N_EXPERTS, logits, NEG_INF)
        m = jnp.max(logits, axis=-1, keepdims=True)
        e = jnp.exp(logits - m)
        probs_out[sl, :] = e / jnp.sum(e, axis=-1, keepdims=True)


def _out_proj(mixed, w_out, h, g_ffn, wr_hi, wr_lo, layer, tm=512):
    t = h.shape[0]
    row = lambda w: pl.BlockSpec((tm, w), lambda i: (i, 0))
    lay = lambda a, b: pl.BlockSpec((None, a, b), lambda i: (layer, 0, 0))
    return pl.pallas_call(
        _out_proj_kernel,
        grid=(t // tm,),
        in_specs=[row(D_MODEL), lay(D_MODEL, D_MODEL), row(D_MODEL), lay(1, D_MODEL),
                  lay(D_MODEL, LANES), lay(D_MODEL, LANES)],
        out_specs=[row(D_MODEL), row(D_MODEL // 2), row(LANES)],
        out_shape=[jax.ShapeDtypeStruct((t, D_MODEL), jnp.float32),
                   jax.ShapeDtypeStruct((t, D_MODEL // 2), jnp.uint32),
                   jax.ShapeDtypeStruct((t, LANES), jnp.float32)],
        compiler_params=_cparams(1),
        name="out_proj_router",
    )(mixed, w_out, h, g_ffn, wr_hi, wr_lo)


def _routing_kernel(p_ref, idx_out, gate_out, tri_ref):
    rows, seq = p_ref.shape
    p = p_ref[...]
    tok = lax.broadcasted_iota(jnp.int32, (rows, seq), 1)

    def count(mask):
        return jnp.sum(jnp.where(mask, 1.0, 0.0), axis=-1, keepdims=True)

    def thr_body(i, thr):
        cand = thr | jnp.left_shift(jnp.int32(1), 30 - i)
        cand_f = lax.bitcast_convert_type(cand, jnp.float32)
        return jnp.where(count(p >= cand_f) >= CAPACITY, cand, thr)

    thr = lax.bitcast_convert_type(lax.fori_loop(0, 31, thr_body, jnp.zeros((rows, 1), jnp.int32)), jnp.float32)
    gt = p > thr
    eq = p == thr
    need = CAPACITY - count(gt)

    def tie_body(i, bound):
        cand = bound | jnp.left_shift(jnp.int32(1), 11 - i)
        return jnp.where(count(jnp.logical_and(eq, tok < cand)) < need, cand, bound)

    bound = lax.fori_loop(0, 12, tie_body, jnp.zeros((rows, 1), jnp.int32))
    sel = jnp.logical_or(gt, jnp.logical_and(eq, tok <= bound))

    @pl.when(pl.program_id(0) == 0)
    def _():
        r_i = lax.broadcasted_iota(jnp.int32, (seq, seq), 0)
        c_i = lax.broadcasted_iota(jnp.int32, (seq, seq), 1)
        tri_ref[...] = jnp.where(r_i < c_i, 1.0, 0.0).astype(jnp.bfloat16)

    self_bf = jnp.where(sel, 1.0, 0.0).astype(jnp.bfloat16)
    pos = jnp.dot(self_bf, tri_ref[...], preferred_element_type=jnp.float32)
    pos = jnp.where(sel, pos, -1.0)

    bf = jnp.bfloat16
    p1 = p.astype(bf).astype(jnp.float32)
    r1 = p - p1
    p2 = r1.astype(bf).astype(jnp.float32)
    p3 = (r1 - p2).astype(bf).astype(jnp.float32)
    t_hi = (tok[0:1] >> 6).astype(jnp.float32)
    t_lo = (tok[0:1] & 63).astype(jnp.float32)
    slot = lax.broadcasted_iota(jnp.int32, (CAPACITY, seq), 0).astype(jnp.float32)
    prow = lax.broadcasted_iota(jnp.int32, (8, seq), 0)
    for r in range(rows):
        onehot = jnp.where(pos[r:r + 1, :] == slot, 1.0, 0.0).astype(bf)
        payload = jnp.where(prow == 0, t_hi, jnp.where(prow == 1, t_lo, jnp.where(
            prow == 2, p1[r:r + 1], jnp.where(prow == 3, p2[r:r + 1], jnp.where(prow == 4, p3[r:r + 1], 0.0)))))
        g = lax.dot_general(payload.astype(bf), onehot, (((1,), (1,)), ((), ())),
                            preferred_element_type=jnp.float32)
        idx_out[r:r + 1, :] = (g[0:1] * 64.0 + g[1:2]).astype(jnp.int32)
        gate_out[r:r + 1, :] = g[2:3] + g[3:4] + g[4:5]


def _routing(probs_t):
    rows, seq = probs_t.shape
    rt = N_EXPERTS
    return pl.pallas_call(
        _routing_kernel,
        grid=(rows // rt,),
        in_specs=[pl.BlockSpec((rt, seq), lambda i: (i, 0))],
        out_specs=[pl.BlockSpec((rt, CAPACITY), lambda i: (i, 0)), pl.BlockSpec((rt, CAPACITY), lambda i: (i, 0))],
        out_shape=[jax.ShapeDtypeStruct((rows, CAPACITY), jnp.int32),
                   jax.ShapeDtypeStruct((rows, CAPACITY), jnp.float32)],
        scratch_shapes=[pltpu.VMEM((seq, seq), jnp.bfloat16)],
        compiler_params=_cparams(1),
        name="expert_choice_routing",
    )(probs_t)


def _gather_kernel(idx_ref, x_ref, o_ref):
    def body(i, c):
        t = idx_ref[0, i]
        o_ref[pl.ds(i, 1), :] = x_ref[pl.ds(t, 1), :]
        return c
    lax.fori_loop(0, CAPACITY, body, 0, unroll=8)


def _gather(idx, x_packed):
    b, e = idx.shape[:2]
    w = x_packed.shape[-1]
    return pl.pallas_call(
        _gather_kernel,
        grid=(b, e),
        in_specs=[pl.BlockSpec((None, None, 1, CAPACITY), lambda i, j: (i, j, 0, 0), memory_space=pltpu.SMEM),
                  pl.BlockSpec((None, SEQ, w), lambda i, j: (i, 0, 0))],
        out_specs=pl.BlockSpec((None, None, CAPACITY, w), lambda i, j: (i, j, 0, 0)),
        out_shape=jax.ShapeDtypeStruct((b, e, CAPACITY, w), jnp.uint32),
        compiler_params=_cparams(2),
        name="moe_gather",
    )(idx, x_packed)


FFN_CHUNKS = 4


def _ffn_kernel(x_ref, gate_ref, wg_ref, wu_ref, wd_ref, o_ref, xs_ref, hs_ref):
    bf = jnp.bfloat16
    f = pl.program_id(2)
    rows = xs_ref.shape[0]

    @pl.when(f == 0)
    def _():
        xp = x_ref[...].reshape(rows, x_ref.shape[2])
        lo = pltpu.unpack_elementwise(xp, index=0, packed_dtype=bf, unpacked_dtype=jnp.float32).astype(bf)
        hi = pltpu.unpack_elementwise(xp, index=1, packed_dtype=bf, unpacked_dtype=jnp.float32).astype(bf)
        xs_ref[...] = jnp.concatenate([lo, hi], axis=1)

    x = xs_ref[...]
    a = jnp.dot(x, wg_ref[...].astype(bf), preferred_element_type=jnp.float32)
    u = jnp.dot(x, wu_ref[...].astype(bf), preferred_element_type=jnp.float32)
    hs_ref[f] = (a / (1.0 + jnp.exp(-a)) * u).astype(bf)

    @pl.when(f == FFN_CHUNKS - 1)
    def _():
        h_all = jnp.concatenate([hs_ref[c] for c in range(FFN_CHUNKS)], axis=1)
        y = jnp.dot(h_all, wd_ref[...].astype(bf), preferred_element_type=jnp.float32)
        out = y * gate_ref[...].reshape(rows, 1)
        o_ref[...] = out.reshape(o_ref.shape).astype(o_ref.dtype)


def _ffn(xe, gate, w_gate, w_up, w_down, layer):
    b, e, c, w = xe.shape
    bb = 4 if b % 4 == 0 else 1
    fc = EXPERT_FF // FFN_CHUNKS
    return pl.pallas_call(
        _ffn_kernel,
        grid=(e, b // bb, FFN_CHUNKS),
        in_specs=[pl.BlockSpec((bb, None, c, w), lambda j, i, f: (i, j, 0, 0)),
                  pl.BlockSpec((bb, None, c, 1), lambda j, i, f: (i, j, 0, 0)),
                  pl.BlockSpec((None, None, D_MODEL, fc), lambda j, i, f: (layer, j, 0, f)),
                  pl.BlockSpec((None, None, D_MODEL, fc), lambda j, i, f: (layer, j, 0, f)),
                  pl.BlockSpec((None, None, EXPERT_FF, D_MODEL), lambda j, i, f: (layer, j, 0, 0))],
        out_specs=pl.BlockSpec((bb, None, c, D_MODEL), lambda j, i, f: (i, j, 0, 0)),
        out_shape=jax.ShapeDtypeStruct((b, e, c, D_MODEL), jnp.bfloat16),
        scratch_shapes=[pltpu.VMEM((bb * c, D_MODEL), jnp.bfloat16),
                        pltpu.VMEM((FFN_CHUNKS, bb * c, fc), jnp.bfloat16)],
        compiler_params=_cparams(3),
        name="expert_ffn",
    )(xe, gate, w_gate, w_up, w_down)


def _combine_kernel(idx_ref, y_ref, h_ref, g_ref, *out_refs, final):
    tm = h_ref.shape[0]
    tok = pl.program_id(1) * tm + lax.broadcasted_iota(jnp.int32, (tm, idx_ref.shape[1]), 0)
    sel = jnp.where(tok == idx_ref[...], 1.0, 0.0).astype(y_ref.dtype)
    h = h_ref[...] + jnp.dot(sel, y_ref[...], preferred_element_type=jnp.float32)
    xn = h * lax.rsqrt(jnp.mean(h * h, axis=-1, keepdims=True) + EPS) * g_ref[...]
    if final:
        out_refs[0][...] = xn.astype(out_refs[0].dtype)
    else:
        out_refs[0][...] = h
        out_refs[1][...] = xn.astype(out_refs[1].dtype)


def _combine(idx, ye, h, g_next, final, tm=256):
    b, n_slots = idx.shape[0], idx.shape[-1]
    row = pl.BlockSpec((None, tm, D_MODEL), lambda i, r: (i, r, 0))
    shape = lambda dt: jax.ShapeDtypeStruct((b, SEQ, D_MODEL), dt)
    return pl.pallas_call(
        functools.partial(_combine_kernel, final=final),
        grid=(b, SEQ // tm),
        in_specs=[pl.BlockSpec((None, 1, n_slots), lambda i, r: (i, 0, 0)),
                  pl.BlockSpec((None, n_slots, D_MODEL), lambda i, r: (i, 0, 0)),
                  row,
                  pl.BlockSpec((1, D_MODEL), lambda i, r: (0, 0))],
        out_specs=[row] if final else [row, row],
        out_shape=[shape(jnp.float32)] if final else [shape(jnp.float32), shape(jnp.bfloat16)],
        compiler_params=_cparams(2),
        name="moe_combine",
    )(idx, ye, h, g_next.reshape(1, D_MODEL))


def _alibi_slopes(n):
    return jnp.exp2(-8.0 * jnp.arange(1, n + 1, dtype=jnp.float32) / n)


def _token_mixer(xn, l, p, tabs):
    t = xn.shape[0]
    b = t // SEQ
    bf = jnp.bfloat16
    groups = _proj_dilated(xn, p["w_all"], l)
    proj = _proj_rest(xn, p["w_all"], l, tm=min(t, 2048))
    proj4 = proj.reshape(b, 1, SEQ, REST_COLS)

    dils = jnp.asarray([dil for _, dil in A_DILATIONS], jnp.float32)
    o_a = _dilated_attention(groups, _alibi_slopes(A_HEADS)[None, :] * dils[:, None])

    qb, kb, vb = _mla_prep(proj, p["g_mla_q"][l], p["g_mla_kv"][l], p["w_mla_uq"][l], p["w_mla_ukv"][l],
                           tabs["mla"], (MLA_NOPE + MLA_ROPE) ** -0.5)
    o_b = _dense_attention(qb, kb, vb,
                           pl.BlockSpec((SEQ, 2 * LANES), lambda i, j: (i, j)),
                           pl.BlockSpec((SEQ, 2 * LANES), lambda i, j: (i, j)),
                           pl.BlockSpec((SEQ, 2 * LANES), lambda i, j: (i, j)), True, "latent_attention")

    qc, kc, vc = _c_prep(proj, p["g_c_q"][l], p["g_c_k"][l], tabs["axial"], HEAD_DIM ** -0.5)
    o_c = _dense_attention(qc, kc, vc,
                           pl.BlockSpec((SEQ, LANES), lambda i, j: (i, j)),
                           pl.BlockSpec((SEQ, LANES), lambda i, j: (i, j // 2)),
                           pl.BlockSpec((SEQ, LANES), lambda i, j: (i, j // 2)), False, "axial_attention")

    o_d = _sink_attention(proj4, _alibi_slopes(D_Q_HEADS), p["sink_d"][l].astype(jnp.float32))

    return _merge(xn, (o_a, o_b, o_c, o_d), p["w_all"], p["b_gate"], p["w_branch"], l)


def _layer(h, xn, l, p, tabs, g_next, final):
    t = h.shape[0]
    b = t // SEQ
    mixed = _token_mixer(xn, l, p, tabs)
    h, x_packed, probs = _out_proj(mixed, p["w_out"], h, p["g_ffn_norm"], p["wr_hi"], p["wr_lo"], l)
    probs_t = jnp.transpose(probs.reshape(b, SEQ, LANES)[:, :, :N_EXPERTS], (0, 2, 1)).reshape(b * N_EXPERTS, SEQ)
    idx, gate = _routing(probs_t)
    xe = _gather(idx.reshape(b, N_EXPERTS, 1, CAPACITY), x_packed.reshape(b, SEQ, D_MODEL // 2))
    ye = _ffn(xe, gate.reshape(b, N_EXPERTS, CAPACITY, 1), p["w_exp_gate"], p["w_exp_up"], p["w_exp_down"], l)
    outs = _combine(idx.reshape(b, 1, N_EXPERTS * CAPACITY), ye.reshape(b, N_EXPERTS * CAPACITY, D_MODEL),
                    h.reshape(b, SEQ, D_MODEL), g_next, final)
    return [o.reshape(t, D_MODEL) for o in outs]


def _prepare_params(w_in, w_mla_uq, w_mla_ukv, w_branch, w_out, w_router, w_exp_gate, w_exp_up, w_exp_down):
    bf = jnp.bfloat16
    p = {}
    p["w_all"] = _relayout_w_in(w_in)
    p["w_mla_uq"] = jnp.stack([_mla_uq_layout(w_mla_uq[l]) for l in range(DEPTH)]).astype(bf)
    p["w_mla_ukv"] = jnp.stack([_mla_ukv_layout(w_mla_ukv[l]) for l in range(DEPTH)]).astype(bf)
    p["w_branch"] = w_branch.astype(bf)
    p["w_out"] = w_out.astype(bf)
    wr = jnp.pad(w_router, ((0, 0), (0, 0), (0, LANES - N_EXPERTS)))
    p["wr_hi"] = wr.astype(bf)
    p["wr_lo"] = (wr - p["wr_hi"].astype(jnp.float32)).astype(bf)
    p["w_exp_gate"] = w_exp_gate
    p["w_exp_up"] = w_exp_up
    p["w_exp_down"] = w_exp_down
    return p


def kernel(x, w_in, b_gate, g_attn_norm, g_ffn_norm, g_mla_q, g_mla_kv, w_mla_uq, w_mla_ukv, g_c_q, g_c_k, sink_d,
           w_branch, w_out, w_router, w_exp_gate, w_exp_up, w_exp_down, g_final):
    b, s, d = x.shape
    assert (s, d) == (SEQ, D_MODEL)
    p = _prepare_params(w_in, w_mla_uq, w_mla_ukv, w_branch, w_out, w_router, w_exp_gate, w_exp_up, w_exp_down)
    p.update(b_gate=b_gate, g_attn_norm=g_attn_norm, g_ffn_norm=g_ffn_norm.reshape(DEPTH, 1, D_MODEL),
             g_mla_q=g_mla_q, g_mla_kv=g_mla_kv, g_c_q=g_c_q, g_c_k=g_c_k, sink_d=sink_d)
    pos = np.arange(SEQ)
    rows, cols = pos // GRID_W, pos % GRID_W
    tabs = {
        "mla": _rope_tables([pos], MLA_ROPE, [MLA_NOPE]),
        "axial": _rope_tables([rows, cols, rows, cols], HEAD_DIM // 2, [0, 32, 64, 96]),
    }
    h = x.reshape(b * s, d)
    xn = _rmsnorm(h, g_attn_norm[0], jnp.bfloat16)
    for l in range(DEPTH - 1):
        h, xn = _layer(h, xn, l, p, tabs, g_attn_norm[l + 1], False)
    (out,) = _layer(h, xn, DEPTH - 1, p, tabs, g_final, True)
    return out.reshape(b, s, d)
```

```python
import functools

import numpy as np
import jax
import jax.numpy as jnp
from jax import lax
from jax.experimental import pallas as pl
from jax.experimental.pallas import tpu as pltpu

D_MODEL = 2048
SEQ = 2048
DEPTH = 2
HEAD_DIM = 64
GRID_W = 64
ROPE_THETA = 10000.0
NEG_INF = -1e30
EPS = 1e-6
N_BRANCH = 4
BRANCH_WIDTH = 512

A_HEADS = 8
A_DILATIONS = ((128, 1), (512, 4), (2048, 16))
N_DIL = 3
MLA_HEADS = 8
MLA_Q_LORA = 512
MLA_KV_LORA = 256
MLA_NOPE = 64
MLA_ROPE = 32
MLA_V = 64
C_Q_HEADS = 8
C_KV_HEADS = 2
D_Q_HEADS = 8
D_KV_HEADS = 2
D_HALF_WINDOW = 128

A_COLS = 3 * N_DIL * A_HEADS * HEAD_DIM
B_COLS = MLA_Q_LORA + MLA_KV_LORA + MLA_ROPE
C_COLS = (C_Q_HEADS + 2 * C_KV_HEADS) * HEAD_DIM
D_COLS = (D_Q_HEADS + 2 * D_KV_HEADS) * HEAD_DIM
GATE_COL0 = A_COLS + B_COLS + C_COLS + D_COLS

N_EXPERTS = 16
EXPERT_FF = D_MODEL // 2
CAPACITY = 2 * SEQ // N_EXPERTS

LANES = 128
N_PAIRS = 4
DEN_LANE = HEAD_DIM
VMEM_LIMIT = 56 * 1024 * 1024

GROUP_COLS = 3 * A_HEADS * HEAD_DIM
REST_COLS = 3072
GATE_COLS = N_BRANCH * D_MODEL
BLK_CQ = 0
BLK_QC = 4
BLK_QD = 8
BLK_CKV = 12
BLK_KPE = 14
BLK_KC = 16
BLK_VC = 18
BLK_KD = 20
BLK_VD = 22


def _cparams(n_grid_dims):
    return pltpu.CompilerParams(dimension_semantics=("arbitrary",) * n_grid_dims, vmem_limit_bytes=VMEM_LIMIT)


def _proj_segments():
    segs = [(0, A_COLS)]
    b0 = A_COLS
    c0 = A_COLS + B_COLS
    d0 = c0 + C_COLS
    segs.append((b0, MLA_Q_LORA))
    segs.append((c0, C_Q_HEADS * HEAD_DIM))
    segs.append((d0, D_Q_HEADS * HEAD_DIM))
    segs.append((b0 + MLA_Q_LORA, MLA_KV_LORA))
    segs += [(None, 64), (b0 + MLA_Q_LORA + MLA_KV_LORA, MLA_ROPE), (None, 32 + LANES)]
    for base, nq, nkv in ((c0, C_Q_HEADS, C_KV_HEADS), (d0, D_Q_HEADS, D_KV_HEADS)):
        for part in range(2):
            for kv in range(nkv):
                start = base + (nq + part * nkv + kv) * HEAD_DIM
                segs += [(start, HEAD_DIM), (start, HEAD_DIM)]
    assert sum(w for _, w in segs) == N_DIL * GROUP_COLS + REST_COLS
    segs.append((GATE_COL0, GATE_COLS))
    return segs


def _relayout_w_in(w_in):
    wt = jnp.transpose(w_in, (0, 2, 1))
    parts = []
    for start, width in _proj_segments():
        if start is None:
            parts.append(jnp.zeros((wt.shape[0], width, wt.shape[2]), jnp.bfloat16))
        else:
            parts.append(wt[:, start:start + width, :].astype(jnp.bfloat16))
    return jnp.concatenate(parts, axis=1)


def _dot_nt(x, w):
    return lax.dot_general(x, w, (((1,), (1,)), ((), ())), preferred_element_type=jnp.float32)


def _mla_uq_layout(w_uq):
    w = w_uq.reshape(MLA_Q_LORA, MLA_HEADS, MLA_NOPE + MLA_ROPE)
    w = jnp.pad(w, ((0, 0), (0, 0), (0, LANES - MLA_NOPE - MLA_ROPE)))
    return w.reshape(MLA_Q_LORA, MLA_HEADS * LANES)


def _mla_ukv_layout(w_ukv):
    w = w_ukv.reshape(MLA_KV_LORA, MLA_HEADS, MLA_NOPE + MLA_V)
    wk = jnp.pad(w[:, :, :MLA_NOPE], ((0, 0), (0, 0), (0, LANES - MLA_NOPE)))
    wv = jnp.pad(w[:, :, MLA_NOPE:], ((0, 0), (0, 0), (0, LANES - MLA_V)))
    return jnp.concatenate([wk.reshape(MLA_KV_LORA, MLA_HEADS * LANES),
                            wv.reshape(MLA_KV_LORA, MLA_HEADS * LANES)], axis=1)


def _rope_tables(pos_list, dim, lane_offsets):
    half = dim // 2
    freqs = ROPE_THETA ** (-np.arange(0, dim, 2, dtype=np.float64) / dim)
    cos_t = np.ones((SEQ, LANES), np.float64)
    sin_a = np.zeros((SEQ, LANES), np.float64)
    sin_b = np.zeros((SEQ, LANES), np.float64)
    for pos, off in zip(pos_list, lane_offsets):
        ang = np.asarray(pos, np.float64)[:, None] * freqs[None, :]
        c, s = np.cos(ang), np.sin(ang)
        cos_t[:, off:off + half] = c
        cos_t[:, off + half:off + dim] = c
        sin_a[:, off:off + half] = -s
        sin_b[:, off + half:off + dim] = s
    return tuple(jnp.asarray(t, jnp.float32) for t in (cos_t, sin_a, sin_b))


def _apply_rope(x, cos_t, sin_a, sin_b, half):
    n = x.shape[-1]
    return (x * cos_t + pltpu.roll(x, n - half, 1) * sin_a + pltpu.roll(x, half, 1) * sin_b)


def _rmsnorm_kernel(x_ref, g_ref, o_ref):
    x = x_ref[...]
    ms = jnp.mean(x * x, axis=-1, keepdims=True)
    o_ref[...] = (x * lax.rsqrt(ms + EPS) * g_ref[...]).astype(o_ref.dtype)


def _rmsnorm(x, g, out_dtype, tm=512):
    t, d = x.shape
    return pl.pallas_call(
        _rmsnorm_kernel,
        grid=(t // tm,),
        in_specs=[pl.BlockSpec((tm, d), lambda i: (i, 0)), pl.BlockSpec((1, d), lambda i: (0, 0))],
        out_specs=pl.BlockSpec((tm, d), lambda i: (i, 0)),
        out_shape=jax.ShapeDtypeStruct((t, d), out_dtype),
        compiler_params=_cparams(1),
        name="rmsnorm",
    )(x, g.reshape(1, d))


def _matmul_kernel(x_ref, w_ref, o_ref):
    o_ref[...] = _dot_nt(x_ref[...], w_ref[...]).astype(o_ref.dtype)


def _proj_rest(x, w_all, layer, tm, tn=1536):
    m, k = x.shape
    blk0 = N_DIL * GROUP_COLS // tn
    return pl.pallas_call(
        _matmul_kernel,
        grid=(m // tm, REST_COLS // tn),
        in_specs=[pl.BlockSpec((tm, k), lambda i, j: (i, 0)),
                  pl.BlockSpec((None, tn, k), lambda i, j: (layer, blk0 + j, 0))],
        out_specs=pl.BlockSpec((tm, tn), lambda i, j: (i, j)),
        out_shape=jax.ShapeDtypeStruct((m, REST_COLS), jnp.bfloat16),
        compiler_params=_cparams(2),
        name="in_proj",
    )(x, w_all)


def _proj_dilated_kernel(x_ref, w0_ref, w1_ref, w2_ref, o0_ref, o1_ref, o2_ref, scr_ref):
    x = x_ref[...]
    for w_ref, o_ref, (_, dil) in zip((w0_ref, w1_ref, w2_ref), (o0_ref, o1_ref, o2_ref), A_DILATIONS):
        acc = _dot_nt(x, w_ref[...])
        if dil == 1:
            o_ref[0] = acc.astype(o_ref.dtype)
        else:
            n_blk = acc.shape[1] // LANES
            for c in range(n_blk):
                scr_ref[c] = acc[:, c * LANES:(c + 1) * LANES]
            for r in range(dil):
                for c in range(n_blk):
                    rows = scr_ref[c, pl.ds(r, SEQ // dil, stride=dil), :]
                    o_ref[r, :, c * LANES:(c + 1) * LANES] = rows.astype(o_ref.dtype)


def _proj_dilated(x, w_all, layer, tn=512):
    t, k = x.shape
    b = t // SEQ
    nc = GROUP_COLS // tn

    def w_spec(g):
        return pl.BlockSpec((None, tn, k), lambda i, c, g=g: (layer, c * N_DIL + g, 0))

    out_specs, out_shape = [], []
    for _, dil in A_DILATIONS:
        ld = SEQ // dil
        out_specs.append(pl.BlockSpec((None, dil, ld, tn), lambda i, c: (i, 0, 0, c)))
        out_shape.append(jax.ShapeDtypeStruct((b, dil, ld, GROUP_COLS), jnp.bfloat16))
    return pl.pallas_call(
        _proj_dilated_kernel,
        grid=(b, nc),
        in_specs=[pl.BlockSpec((SEQ, k), lambda i, c: (i, 0)), w_spec(0), w_spec(1), w_spec(2)],
        out_specs=out_specs,
        out_shape=out_shape,
        scratch_shapes=[pltpu.VMEM((tn // LANES, SEQ, LANES), jnp.float32)],
        compiler_params=_cparams(2),
        name="in_proj_dilated",
    )(x, w_all, w_all, w_all)


BANDED_GROUP_SCORE_ELEMS = 256 * 1024


def _banded_tiles(half_window, length):
    return 2 * half_window, min(4 * half_window, length)


def _banded_sequences(slopes, sinks, q_ref, k_ref, v_ref, bias_ref, write, *, length, half_window, dil):
    use_sink = sinks is not None
    w = half_window
    tq, tk = _banded_tiles(w, length)
    nq = length // tq
    key_start = [min(max(qb * tq - w, 0), length - tk) for qb in range(nq)]
    offsets = sorted({key_start[qb] - qb * tq for qb in range(nq)})

    lane = lax.broadcasted_iota(jnp.int32, (1, LANES), 1)
    half0 = lane < HEAD_DIM
    delta = lax.broadcasted_iota(jnp.int32, (tq, tk), 1) - lax.broadcasted_iota(jnp.int32, (tq, tk), 0)
    for oi, off in enumerate(offsets):
        rel = jnp.abs(delta + off)
        dist = rel.astype(jnp.float32)
        for i in range(2):
            bias_ref[2 * oi + i] = jnp.where(rel <= w, -slopes[i] * dist, NEG_INF)

    scale = jnp.asarray(HEAD_DIM ** -0.5, q_ref.dtype)
    blocks = [(r, qb) for r in range(dil) for qb in range(nq)]
    group = max(2, BANDED_GROUP_SCORE_ELEMS // (2 * tq * tk))
    for g0 in range(0, len(blocks), group):
        grp = blocks[g0:g0 + group]
        scores = []
        for r, qb in grp:
            qs, ks = qb * tq, key_start[qb]
            oi = offsets.index(ks - qs)
            q = q_ref[r, qs:qs + tq, :] * scale
            kw = k_ref[r, ks:ks + tk, :]
            for i in range(2):
                qi = jnp.where(half0 if i == 0 else jnp.logical_not(half0), q, jnp.zeros_like(q))
                scores.append(_dot_nt(qi, kw) + bias_ref[2 * oi + i])
        probs, maxes, dens = [], [], []
        for n, s in enumerate(scores):
            m = jnp.max(s, axis=-1, keepdims=True)
            if use_sink:
                sk = sinks[n % 2]
                m = jnp.maximum(m, sk)
            p = jnp.exp(s - m)
            den = jnp.sum(p, axis=-1, keepdims=True)
            if use_sink:
                den = den + jnp.exp(sk - m)
            probs.append(p.astype(v_ref.dtype))
            maxes.append(m)
            dens.append(den)
        for n, (r, qb) in enumerate(grp):
            qs, ks = qb * tq, key_start[qb]
            vw = v_ref[r, ks:ks + tk, :]
            outs = [jnp.dot(probs[2 * n + i], vw, preferred_element_type=jnp.float32) / dens[2 * n + i]
                    for i in range(2)]
            rows = pl.ds(qs, tq) if dil == 1 else pl.ds(r + qs * dil, tq, stride=dil)
            lses = [maxes[2 * n + i] + jnp.log(dens[2 * n + i]) for i in range(2)]
            write(rows, jnp.where(half0, outs[0], outs[1]), jnp.where(half0, lses[0], lses[1]))


def _bias_scratch(half_window, length):
    tq, tk = _banded_tiles(half_window, length)
    return pltpu.VMEM((2 if tk == length else 6, tq, tk), jnp.float32)


def _sink_kernel(slope_ref, sink_ref, q_ref, k_ref, v_ref, o_ref, bias_ref):
    hp = pl.program_id(1)

    def write(rows, o, lse):
        del lse
        o_ref[rows, :] = o.astype(o_ref.dtype)

    _banded_sequences([slope_ref[2 * hp + i] for i in range(2)], [sink_ref[2 * hp + i] for i in range(2)],
                      q_ref, k_ref, v_ref, bias_ref, write, length=SEQ, half_window=D_HALF_WINDOW, dil=1)


def _sink_attention(proj4, slopes, sinks):
    b = proj4.shape[0]
    smem = pl.BlockSpec(memory_space=pltpu.SMEM)

    def spec(blk_fn):
        return pl.BlockSpec((None, 1, SEQ, LANES), lambda i, j: (i, 0, 0, blk_fn(j)))

    return pl.pallas_call(
        _sink_kernel,
        grid=(b, N_PAIRS),
        in_specs=[smem, smem, spec(lambda j: BLK_QD + j), spec(lambda j: BLK_KD + j // 2),
                  spec(lambda j: BLK_VD + j // 2)],
        out_specs=pl.BlockSpec((SEQ, LANES), lambda i, j: (i, j)),
        out_shape=jax.ShapeDtypeStruct((b * SEQ, N_PAIRS * LANES), jnp.bfloat16),
        scratch_shapes=[_bias_scratch(D_HALF_WINDOW, SEQ)],
        compiler_params=_cparams(2),
        name="banded_sink",
    )(slopes, sinks, proj4, proj4, proj4)


def _dilated_kernel(slope_ref, *refs):
    qkv_refs = refs[:9]
    o_ref = refs[9]
    bias_refs = refs[10:13]
    o_stage, lse_stage = refs[13], refs[14]
    hp = pl.program_id(1)
    for g, (window, dil) in enumerate(A_DILATIONS):
        def write(rows, o, lse, g=g):
            o_stage[g, rows, :] = o
            lse_stage[g, rows, :] = lse

        _banded_sequences([slope_ref[g, 2 * hp + i] for i in range(2)], None, *qkv_refs[3 * g:3 * g + 3],
                          bias_refs[g], write, length=SEQ // dil, half_window=(window // 2) // dil, dil=dil)
    la, lb, lc = lse_stage[0], lse_stage[1], lse_stage[2]
    m = jnp.maximum(jnp.maximum(la, lb), lc)
    ea, eb, ec = jnp.exp(la - m), jnp.exp(lb - m), jnp.exp(lc - m)
    num = ea * o_stage[0] + eb * o_stage[1] + ec * o_stage[2]
    o_ref[...] = (num / (ea + eb + ec)).astype(o_ref.dtype)


def _dilated_attention(groups, slopes):
    b = groups[0].shape[0]
    in_specs = [pl.BlockSpec(memory_space=pltpu.SMEM)]
    operands = [slopes]
    scratch = []
    for g, (window, dil) in enumerate(A_DILATIONS):
        ld = SEQ // dil
        for s in range(3):
            in_specs.append(pl.BlockSpec((None, dil, ld, LANES), lambda i, j, s=s: (i, 0, 0, s * N_PAIRS + j)))
            operands.append(groups[g])
        scratch.append(_bias_scratch((window // 2) // dil, ld))
    scratch += [pltpu.VMEM((N_DIL, SEQ, LANES), jnp.float32)] * 2
    return pl.pallas_call(
        _dilated_kernel,
        grid=(b, N_PAIRS),
        in_specs=in_specs,
        out_specs=pl.BlockSpec((SEQ, LANES), lambda i, j: (i, j)),
        out_shape=jax.ShapeDtypeStruct((b * SEQ, N_PAIRS * LANES), jnp.bfloat16),
        scratch_shapes=scratch,
        compiler_params=_cparams(2),
        name="banded_dilated",
    )(*operands)


def _mla_prep_kernel(cq_ref, ckv_ref, kpe_ref, gq_ref, gkv_ref, wuq_ref, wukv_ref, cos_ref, sa_ref, sb_ref,
                     q_out, k_out, v_out, *, scale):
    def norm(x, g):
        x = x.astype(jnp.float32)
        return (x * lax.rsqrt(jnp.mean(x * x, axis=-1, keepdims=True) + EPS) * g).astype(jnp.bfloat16)

    cos_t, sin_a, sin_b = cos_ref[...], sa_ref[...], sb_ref[...]
    half = MLA_ROPE // 2
    q = jnp.dot(norm(cq_ref[...], gq_ref[...]), wuq_ref[...], preferred_element_type=jnp.float32)
    kv = jnp.dot(norm(ckv_ref[...], gkv_ref[...]), wukv_ref[...], preferred_element_type=jnp.float32)
    kpe = _apply_rope(kpe_ref[...].astype(jnp.float32), cos_t, sin_a, sin_b, half)
    for h in range(MLA_HEADS):
        sl = slice(h * LANES, (h + 1) * LANES)
        q_out[:, sl] = (_apply_rope(q[:, sl], cos_t, sin_a, sin_b, half) * scale).astype(q_out.dtype)
        k_out[:, sl] = (kv[:, sl] + kpe).astype(k_out.dtype)
    lane = lax.broadcasted_iota(jnp.int32, (1, MLA_HEADS * LANES), 1)
    ones_lane = jnp.where(lane % LANES == DEN_LANE, 1.0, 0.0)
    v_out[...] = (kv[:, MLA_HEADS * LANES:] + ones_lane).astype(v_out.dtype)


def _mla_prep(proj, g_q, g_kv, w_uq, w_ukv, tables, scale, tm=512):
    t = proj.shape[0]
    nblk = SEQ // tm
    tab_spec = pl.BlockSpec((tm, LANES), lambda i: (i % nblk, 0))
    nq, nk, nv = MLA_HEADS * LANES, MLA_HEADS * LANES, MLA_HEADS * LANES
    return pl.pallas_call(
        functools.partial(_mla_prep_kernel, scale=scale),
        grid=(t // tm,),
        in_specs=[pl.BlockSpec((tm, MLA_Q_LORA), lambda i: (i, BLK_CQ * LANES // MLA_Q_LORA)),
                  pl.BlockSpec((tm, MLA_KV_LORA), lambda i: (i, BLK_CKV * LANES // MLA_KV_LORA)),
                  pl.BlockSpec((tm, LANES), lambda i: (i, BLK_KPE)),
                  pl.BlockSpec((1, MLA_Q_LORA), lambda i: (0, 0)),
                  pl.BlockSpec((1, MLA_KV_LORA), lambda i: (0, 0)),
                  pl.BlockSpec(w_uq.shape, lambda i: (0, 0)),
                  pl.BlockSpec(w_ukv.shape, lambda i: (0, 0)),
                  tab_spec, tab_spec, tab_spec],
        out_specs=[pl.BlockSpec((tm, nq), lambda i: (i, 0)),
                   pl.BlockSpec((tm, nk), lambda i: (i, 0)),
                   pl.BlockSpec((tm, nv), lambda i: (i, 0))],
        out_shape=[jax.ShapeDtypeStruct((t, nq), jnp.bfloat16),
                   jax.ShapeDtypeStruct((t, nk), jnp.bfloat16),
                   jax.ShapeDtypeStruct((t, nv), jnp.bfloat16)],
        compiler_params=_cparams(1),
        name="mla_prep",
    )(proj, proj, proj, g_q.reshape(1, -1), g_kv.reshape(1, -1), w_uq, w_ukv, *tables)


def _c_prep_kernel(q_ref, k_ref, v_ref, gq_ref, gk_ref, cos_ref, sa_ref, sb_ref, q_out, k_out, v_out, *, scale):
    lane = lax.broadcasted_iota(jnp.int32, (1, LANES), 1)
    half0 = lane < HEAD_DIM
    cos_t, sin_a, sin_b = cos_ref[...], sa_ref[...], sb_ref[...]
    ones_lane = jnp.where(lane == DEN_LANE, 1.0, 0.0).astype(v_out.dtype)
    for p in range(v_ref.shape[1] // LANES):
        sl = slice(p * LANES, (p + 1) * LANES)
        v_out[:, sl] = jnp.where(half0, v_ref[:, sl], ones_lane)

    def prep(x, g):
        x = x.astype(jnp.float32)
        sq = x * x
        s0 = jnp.sum(jnp.where(half0, sq, 0.0), axis=-1, keepdims=True)
        s1 = jnp.sum(jnp.where(half0, 0.0, sq), axis=-1, keepdims=True)
        inv = jnp.where(half0, lax.rsqrt(s0 / HEAD_DIM + EPS), lax.rsqrt(s1 / HEAD_DIM + EPS))
        return _apply_rope(x * inv * g, cos_t, sin_a, sin_b, HEAD_DIM // 4)

    for p in range(q_ref.shape[1] // LANES):
        sl = slice(p * LANES, (p + 1) * LANES)
        q_out[:, sl] = (prep(q_ref[:, sl], gq_ref[...]) * scale).astype(q_out.dtype)
    for p in range(k_ref.shape[1] // LANES):
        sl = slice(p * LANES, (p + 1) * LANES)
        k_out[:, sl] = prep(k_ref[:, sl], gk_ref[...]).astype(k_out.dtype)


def _c_prep(proj, g_q, g_k, tables, scale, tm=512):
    t = proj.shape[0]
    nblk = SEQ // tm
    tab_spec = pl.BlockSpec((tm, LANES), lambda i: (i % nblk, 0))
    qw, kw = N_PAIRS * LANES, C_KV_HEADS * LANES
    g_q2 = jnp.tile(g_q.reshape(1, HEAD_DIM), (1, 2))
    g_k2 = jnp.tile(g_k.reshape(1, HEAD_DIM), (1, 2))
    return pl.pallas_call(
        functools.partial(_c_prep_kernel, scale=scale),
        grid=(t // tm,),
        in_specs=[pl.BlockSpec((tm, qw), lambda i: (i, BLK_QC * LANES // qw)),
                  pl.BlockSpec((tm, kw), lambda i: (i, BLK_KC * LANES // kw)),
                  pl.BlockSpec((tm, kw), lambda i: (i, BLK_VC * LANES // kw)),
                  pl.BlockSpec((1, LANES), lambda i: (0, 0)),
                  pl.BlockSpec((1, LANES), lambda i: (0, 0)),
                  tab_spec, tab_spec, tab_spec],
        out_specs=[pl.BlockSpec((tm, qw), lambda i: (i, 0)), pl.BlockSpec((tm, kw), lambda i: (i, 0)),
                   pl.BlockSpec((tm, kw), lambda i: (i, 0))],
        out_shape=[jax.ShapeDtypeStruct((t, qw), jnp.bfloat16), jax.ShapeDtypeStruct((t, kw), jnp.bfloat16),
                   jax.ShapeDtypeStruct((t, kw), jnp.bfloat16)],
        compiler_params=_cparams(1),
        name="axial_prep",
    )(proj, proj, proj, g_q2, g_k2, *tables)


def _attn_prep_kernel(*refs, mla_scale, c_scale):
    _mla_prep_kernel(*refs[0:10], *refs[18:21], scale=mla_scale)
    _c_prep_kernel(*refs[10:18], *refs[21:24], scale=c_scale)


def _attn_prep(proj, p, l, tabs, mla_scale, c_scale, tm=512):
    t = proj.shape[0]
    nblk = SEQ // tm
    tab = pl.BlockSpec((tm, LANES), lambda i: (i % nblk, 0))
    row = lambda w, blk: pl.BlockSpec((tm, w), lambda i: (i, blk * LANES // w))
    const = lambda shape: pl.BlockSpec(shape, lambda i: (0,) * len(shape))
    qw, kw, nh = N_PAIRS * LANES, C_KV_HEADS * LANES, MLA_HEADS * LANES
    w_uq, w_ukv = p["w_mla_uq"][l], p["w_mla_ukv"][l]
    g_q2 = jnp.tile(p["g_c_q"][l].reshape(1, HEAD_DIM), (1, 2))
    g_k2 = jnp.tile(p["g_c_k"][l].reshape(1, HEAD_DIM), (1, 2))
    out = lambda w: pl.BlockSpec((tm, w), lambda i: (i, 0))
    shp = lambda w: jax.ShapeDtypeStruct((t, w), jnp.bfloat16)
    return pl.pallas_call(
        functools.partial(_attn_prep_kernel, mla_scale=mla_scale, c_scale=c_scale),
        grid=(t // tm,),
        in_specs=[row(MLA_Q_LORA, BLK_CQ), row(MLA_KV_LORA, BLK_CKV), row(LANES, BLK_KPE),
                  const((1, MLA_Q_LORA)), const((1, MLA_KV_LORA)), const(w_uq.shape), const(w_ukv.shape),
                  tab, tab, tab,
                  row(qw, BLK_QC), row(kw, BLK_KC), row(kw, BLK_VC), const((1, LANES)), const((1, LANES)),
                  tab, tab, tab],
        out_specs=[out(nh), out(nh), out(nh), out(qw), out(kw), out(kw)],
        out_shape=[shp(nh), shp(nh), shp(nh), shp(qw), shp(kw), shp(kw)],
        compiler_params=_cparams(1),
        name="attn_prep",
    )(proj, proj, proj, p["g_mla_q"][l].reshape(1, -1), p["g_mla_kv"][l].reshape(1, -1), w_uq, w_ukv, *tabs["mla"],
      proj, proj, proj, g_q2, g_k2, *tabs["axial"])


DENSE_TILES_PER_STEP = 8


def _dense_kernel(q_ref, k_ref, v_ref, o_ref, *, tq, head_blocks):
    lane = lax.broadcasted_iota(jnp.int32, (1, LANES), 1)
    half0 = lane < HEAD_DIM
    seq = q_ref.shape[0]

    def head_operand(ref, i):
        return ref[:, i * LANES:(i + 1) * LANES] if head_blocks else ref[...]

    def body(t, carry):
        starts = [pl.multiple_of((t * DENSE_TILES_PER_STEP + u) * tq, tq) for u in range(DENSE_TILES_PER_STEP)]
        scores = []
        for qs in starts:
            q = q_ref[pl.ds(qs, tq), :]
            for i in range(2):
                if head_blocks:
                    qi = q[:, i * LANES:(i + 1) * LANES]
                else:
                    qi = jnp.where(half0 if i == 0 else jnp.logical_not(half0), q, jnp.zeros_like(q))
                scores.append(_dot_nt(qi, head_operand(k_ref, i)))
        for u, qs in enumerate(starts):
            outs = []
            for i in range(2):
                s = scores[2 * u + i]
                vi = head_operand(v_ref, i)
                m = jnp.max(s, axis=-1, keepdims=True)
                p = jnp.exp((s - m).astype(vi.dtype))
                pv = jnp.dot(p, vi, preferred_element_type=jnp.float32)
                outs.append(pv * (1.0 / pv[:, DEN_LANE:DEN_LANE + 1]))
            o_ref[pl.ds(qs, tq), :] = jnp.where(half0, outs[0], pltpu.roll(outs[1], HEAD_DIM, 1)).astype(o_ref.dtype)
        return carry

    lax.fori_loop(0, seq // (tq * DENSE_TILES_PER_STEP), body, 0)


def _dense_attention(q, k, v, q_spec, k_spec, v_spec, head_blocks, name, tq=256):
    t = q.shape[0]
    b = t // SEQ
    return pl.pallas_call(
        functools.partial(_dense_kernel, tq=tq, head_blocks=head_blocks),
        grid=(b, N_PAIRS),
        in_specs=[q_spec, k_spec, v_spec],
        out_specs=pl.BlockSpec((SEQ, LANES), lambda i, j: (i, j)),
        out_shape=jax.ShapeDtypeStruct((t, N_PAIRS * LANES), jnp.bfloat16),
        compiler_params=_cparams(2),
        name=name,
    )(q, k, v)


def _merge_kernel(xn_ref, oa_ref, ob_ref, oc_ref, od_ref, wga_ref, wgb_ref, wgc_ref, wgd_ref, bg_ref, wb_ref,
                  out_ref):
    xn = xn_ref[...]
    acc = None
    for j, (o_ref, wg_ref) in enumerate(((oa_ref, wga_ref), (ob_ref, wgb_ref), (oc_ref, wgc_ref), (od_ref, wgd_ref))):
        z = _dot_nt(xn, wg_ref[...]) + bg_ref[j:j + 1, :]
        g = 1.0 / (1.0 + jnp.exp(-z))
        u = jnp.dot(o_ref[...], wb_ref[j], preferred_element_type=jnp.float32)
        acc = g * u if acc is None else acc + g * u
    out_ref[...] = acc.astype(out_ref.dtype)


def _merge(xn, o_list, w_all, b_gate, w_branch, layer, tm=1024, tc=512):
    t = xn.shape[0]
    nc = D_MODEL // tc
    gate_blk0 = (N_DIL * GROUP_COLS + REST_COLS) // tc
    o_spec = pl.BlockSpec((tm, BRANCH_WIDTH), lambda c, i: (i, 0))

    def wg_spec(j):
        return pl.BlockSpec((None, tc, D_MODEL), lambda c, i, j=j: (layer, gate_blk0 + j * nc + c, 0))

    return pl.pallas_call(
        _merge_kernel,
        grid=(nc, t // tm),
        in_specs=[pl.BlockSpec((tm, D_MODEL), lambda c, i: (i, 0))] + [o_spec] * 4
        + [wg_spec(0), wg_spec(1), wg_spec(2), wg_spec(3),
           pl.BlockSpec((None, N_BRANCH, tc), lambda c, i: (layer, 0, c)),
           pl.BlockSpec((None, N_BRANCH, BRANCH_WIDTH, tc), lambda c, i: (layer, 0, 0, c))],
        out_specs=pl.BlockSpec((tm, tc), lambda c, i: (i, c)),
        out_shape=jax.ShapeDtypeStruct((t, D_MODEL), jnp.bfloat16),
        compiler_params=_cparams(2),
        name="gated_merge",
    )(xn, *o_list, w_all, w_all, w_all, w_all, b_gate, w_branch)


OUT_PROJ_SUBTILES = 2


def _out_proj_kernel(mixed_ref, w_ref, h_ref, g_ref, wr_hi_ref, wr_lo_ref, h_out, xp_out, probs_out):
    rows = mixed_ref.shape[0] // OUT_PROJ_SUBTILES
    half = D_MODEL // 2
    lane = lax.broadcasted_iota(jnp.int32, (1, LANES), 1)
    for t in range(OUT_PROJ_SUBTILES):
        sl = slice(t * rows, (t + 1) * rows)
        h = h_ref[sl, :] + jnp.dot(mixed_ref[sl, :], w_ref[...], preferred_element_type=jnp.float32)
        h_out[sl, :] = h
        xn = h * lax.rsqrt(jnp.mean(h * h, axis=-1, keepdims=True) + EPS) * g_ref[...]
        xp_out[sl, :] = pltpu.pack_elementwise([xn[:, :half], xn[:, half:]], packed_dtype=jnp.bfloat16)
        x_hi = xn.astype(jnp.bfloat16)
        x_lo = (xn - x_hi.astype(jnp.float32)).astype(jnp.bfloat16)
        logits = (jnp.dot(x_hi, wr_hi_ref[...], preferred_element_type=jnp.float32)
                  + jnp.dot(x_lo, wr_hi_ref[...], preferred_element_type=jnp.float32)
                  + jnp.dot(x_hi, wr_lo_ref[...], preferred_element_type=jnp.float32))
        logits = jnp.where(lane < N_EXPERTS, logits, NEG_INF)
        m = jnp.max(logits, axis=-1, keepdims=True)
        e = jnp.exp(logits - m)
        probs_out[sl, :] = e / jnp.sum(e, axis=-1, keepdims=True)


def _out_proj(mixed, w_out, h, g_ffn, wr_hi, wr_lo, layer, tm=512):
    t = h.shape[0]
    row = lambda w: pl.BlockSpec((tm, w), lambda i: (i, 0))
    lay = lambda a, b: pl.BlockSpec((None, a, b), lambda i: (layer, 0, 0))
    return pl.pallas_call(
        _out_proj_kernel,
        grid=(t // tm,),
        in_specs=[row(D_MODEL), lay(D_MODEL, D_MODEL), row(D_MODEL), lay(1, D_MODEL),
                  lay(D_MODEL, LANES), lay(D_MODEL, LANES)],
        out_specs=[row(D_MODEL), row(D_MODEL // 2), row(LANES)],
        out_shape=[jax.ShapeDtypeStruct((t, D_MODEL), jnp.float32),
                   jax.ShapeDtypeStruct((t, D_MODEL // 2), jnp.uint32),
                   jax.ShapeDtypeStruct((t, LANES), jnp.float32)],
        compiler_params=_cparams(1),
        name="out_proj_router",
    )(mixed, w_out, h, g_ffn, wr_hi, wr_lo)


def _routing_kernel(p_ref, idx_out, gate_out, tri_ref):
    rows, seq = p_ref.shape
    p = p_ref[...]
    tok = lax.broadcasted_iota(jnp.int32, (rows, seq), 1)

    def count(mask):
        return jnp.sum(jnp.where(mask, 1.0, 0.0), axis=-1, keepdims=True)

    def thr_body(i, thr):
        cand = thr | jnp.left_shift(jnp.int32(1), 30 - i)
        cand_f = lax.bitcast_convert_type(cand, jnp.float32)
        return jnp.where(count(p >= cand_f) >= CAPACITY, cand, thr)

    thr = lax.bitcast_convert_type(lax.fori_loop(0, 31, thr_body, jnp.zeros((rows, 1), jnp.int32)), jnp.float32)
    gt = p > thr
    eq = p == thr
    need = CAPACITY - count(gt)

    def tie_body(i, bound):
        cand = bound | jnp.left_shift(jnp.int32(1), 11 - i)
        return jnp.where(count(jnp.logical_and(eq, tok < cand)) < need, cand, bound)

    bound = lax.fori_loop(0, 12, tie_body, jnp.zeros((rows, 1), jnp.int32))
    sel = jnp.logical_or(gt, jnp.logical_and(eq, tok <= bound))

    @pl.when(pl.program_id(0) == 0)
    def _():
        r_i = lax.broadcasted_iota(jnp.int32, (seq, seq), 0)
        c_i = lax.broadcasted_iota(jnp.int32, (seq, seq), 1)
        tri_ref[...] = jnp.where(r_i < c_i, 1.0, 0.0).astype(jnp.bfloat16)

    self_bf = jnp.where(sel, 1.0, 0.0).astype(jnp.bfloat16)
    pos = jnp.dot(self_bf, tri_ref[...], preferred_element_type=jnp.float32)
    pos = jnp.where(sel, pos, -1.0)

    bf = jnp.bfloat16
    p1 = p.astype(bf).astype(jnp.float32)
    r1 = p - p1
    p2 = r1.astype(bf).astype(jnp.float32)
    p3 = (r1 - p2).astype(bf).astype(jnp.float32)
    t_hi = (tok[0:1] >> 6).astype(jnp.float32)
    t_lo = (tok[0:1] & 63).astype(jnp.float32)
    slot = lax.broadcasted_iota(jnp.int32, (CAPACITY, seq), 0).astype(jnp.float32)
    prow = lax.broadcasted_iota(jnp.int32, (8, seq), 0)
    for r in range(rows):
        onehot = jnp.where(pos[r:r + 1, :] == slot, 1.0, 0.0).astype(bf)
        payload = jnp.where(prow == 0, t_hi, jnp.where(prow == 1, t_lo, jnp.where(
            prow == 2, p1[r:r + 1], jnp.where(prow == 3, p2[r:r + 1], jnp.where(prow == 4, p3[r:r + 1], 0.0)))))
        g = lax.dot_general(payload.astype(bf), onehot, (((1,), (1,)), ((), ())),
                            preferred_element_type=jnp.float32)
        idx_out[r:r + 1, :] = (g[0:1] * 64.0 + g[1:2]).astype(jnp.int32)
        gate_out[r:r + 1, :] = g[2:3] + g[3:4] + g[4:5]


def _routing(probs_t):
    rows, seq = probs_t.shape
    rt = N_EXPERTS
    return pl.pallas_call(
        _routing_kernel,
        grid=(rows // rt,),
        in_specs=[pl.BlockSpec((rt, seq), lambda i: (i, 0))],
        out_specs=[pl.BlockSpec((rt, CAPACITY), lambda i: (i, 0)), pl.BlockSpec((rt, CAPACITY), lambda i: (i, 0))],
        out_shape=[jax.ShapeDtypeStruct((rows, CAPACITY), jnp.int32),
                   jax.ShapeDtypeStruct((rows, CAPACITY), jnp.float32)],
        scratch_shapes=[pltpu.VMEM((seq, seq), jnp.bfloat16)],
        compiler_params=_cparams(1),
        name="expert_choice_routing",
    )(probs_t)


def _gather_kernel(idx_ref, x_ref, o_ref):
    def body(i, c):
        t = idx_ref[0, i]
        o_ref[pl.ds(i, 1), :] = x_ref[pl.ds(t, 1), :]
        return c
    lax.fori_loop(0, CAPACITY, body, 0, unroll=8)


def _gather(idx, x_packed):
    b, e = idx.shape[:2]
    w = x_packed.shape[-1]
    return pl.pallas_call(
        _gather_kernel,
        grid=(b, e),
        in_specs=[pl.BlockSpec((None, None, 1, CAPACITY), lambda i, j: (i, j, 0, 0), memory_space=pltpu.SMEM),
                  pl.BlockSpec((None, SEQ, w), lambda i, j: (i, 0, 0))],
        out_specs=pl.BlockSpec((None, None, CAPACITY, w), lambda i, j: (i, j, 0, 0)),
        out_shape=jax.ShapeDtypeStruct((b, e, CAPACITY, w), jnp.uint32),
        compiler_params=_cparams(2),
        name="moe_gather",
    )(idx, x_packed)


FFN_CHUNKS = 4


def _ffn_kernel(x_ref, gate_ref, wg_ref, wu_ref, wd_ref, o_ref, xs_ref, hs_ref):
    bf = jnp.bfloat16
    f = pl.program_id(2)
    rows = xs_ref.shape[0]

    @pl.when(f == 0)
    def _():
        xp = x_ref[...].reshape(rows, x_ref.shape[2])
        lo = pltpu.unpack_elementwise(xp, index=0, packed_dtype=bf, unpacked_dtype=jnp.float32).astype(bf)
        hi = pltpu.unpack_elementwise(xp, index=1, packed_dtype=bf, unpacked_dtype=jnp.float32).astype(bf)
        xs_ref[...] = jnp.concatenate([lo, hi], axis=1)

    x = xs_ref[...]
    a = jnp.dot(x, wg_ref[...].astype(bf), preferred_element_type=jnp.float32)
    u = jnp.dot(x, wu_ref[...].astype(bf), preferred_element_type=jnp.float32)
    hs_ref[f] = (a / (1.0 + jnp.exp(-a)) * u).astype(bf)

    @pl.when(f == FFN_CHUNKS - 1)
    def _():
        h_all = jnp.concatenate([hs_ref[c] for c in range(FFN_CHUNKS)], axis=1)
        y = jnp.dot(h_all, wd_ref[...].astype(bf), preferred_element_type=jnp.float32)
        out = y * gate_ref[...].reshape(rows, 1)
        o_ref[...] = out.reshape(o_ref.shape).astype(o_ref.dtype)


def _ffn(xe, gate, w_gate, w_up, w_down, layer):
    b, e, c, w = xe.shape
    bb = 4 if b % 4 == 0 else 1
    fc = EXPERT_FF // FFN_CHUNKS
    return pl.pallas_call(
        _ffn_kernel,
        grid=(e, b // bb, FFN_CHUNKS),
        in_specs=[pl.BlockSpec((bb, None, c, w), lambda j, i, f: (i, j, 0, 0)),
                  pl.BlockSpec((bb, None, c, 1), lambda j, i, f: (i, j, 0, 0)),
                  pl.BlockSpec((None, None, D_MODEL, fc), lambda j, i, f: (layer, j, 0, f)),
                  pl.BlockSpec((None, None, D_MODEL, fc), lambda j, i, f: (layer, j, 0, f)),
                  pl.BlockSpec((None, None, EXPERT_FF, D_MODEL), lambda j, i, f: (layer, j, 0, 0))],
        out_specs=pl.BlockSpec((bb, None, c, D_MODEL), lambda j, i, f: (i, j, 0, 0)),
        out_shape=jax.ShapeDtypeStruct((b, e, c, D_MODEL), jnp.bfloat16),
        scratch_shapes=[pltpu.VMEM((bb * c, D_MODEL), jnp.bfloat16),
                        pltpu.VMEM((FFN_CHUNKS, bb * c, fc), jnp.bfloat16)],
        compiler_params=_cparams(3),
        name="expert_ffn",
    )(xe, gate, w_gate, w_up, w_down)


def _combine_kernel(idx_ref, y_ref, h_ref, g_ref, *out_refs, final):
    tm = h_ref.shape[0]
    tok = pl.program_id(1) * tm + lax.broadcasted_iota(jnp.int32, (tm, idx_ref.shape[1]), 0)
    sel = jnp.where(tok == idx_ref[...], 1.0, 0.0).astype(y_ref.dtype)
    h = h_ref[...] + jnp.dot(sel, y_ref[...], preferred_element_type=jnp.float32)
    xn = h * lax.rsqrt(jnp.mean(h * h, axis=-1, keepdims=True) + EPS) * g_ref[...]
    if final:
        out_refs[0][...] = xn.astype(out_refs[0].dtype)
    else:
        out_refs[0][...] = h
        out_refs[1][...] = xn.astype(out_refs[1].dtype)


def _combine(idx, ye, h, g_next, final, tm=256):
    b, n_slots = idx.shape[0], idx.shape[-1]
    row = pl.BlockSpec((None, tm, D_MODEL), lambda i, r: (i, r, 0))
    shape = lambda dt: jax.ShapeDtypeStruct((b, SEQ, D_MODEL), dt)
    return pl.pallas_call(
        functools.partial(_combine_kernel, final=final),
        grid=(b, SEQ // tm),
        in_specs=[pl.BlockSpec((None, 1, n_slots), lambda i, r: (i, 0, 0)),
                  pl.BlockSpec((None, n_slots, D_MODEL), lambda i, r: (i, 0, 0)),
                  row,
                  pl.BlockSpec((1, D_MODEL), lambda i, r: (0, 0))],
        out_specs=[row] if final else [row, row],
        out_shape=[shape(jnp.float32)] if final else [shape(jnp.float32), shape(jnp.bfloat16)],
        compiler_params=_cparams(2),
        name="moe_combine",
    )(idx, ye, h, g_next.reshape(1, D_MODEL))


def _alibi_slopes(n):
    return jnp.exp2(-8.0 * jnp.arange(1, n + 1, dtype=jnp.float32) / n)


def _token_mixer(xn, l, p, tabs):
    t = xn.shape[0]
    b = t // SEQ
    bf = jnp.bfloat16
    groups = _proj_dilated(xn, p["w_all"], l)
    proj = _proj_rest(xn, p["w_all"], l, tm=min(t, 2048))
    proj4 = proj.reshape(b, 1, SEQ, REST_COLS)

    dils = jnp.asarray([dil for _, dil in A_DILATIONS], jnp.float32)
    o_a = _dilated_attention(groups, _alibi_slopes(A_HEADS)[None, :] * dils[:, None])

    qb, kb, vb, qc, kc, vc = _attn_prep(proj, p, l, tabs, (MLA_NOPE + MLA_ROPE) ** -0.5, HEAD_DIM ** -0.5)
    o_b = _dense_attention(qb, kb, vb,
                           pl.BlockSpec((SEQ, 2 * LANES), lambda i, j: (i, j)),
                           pl.BlockSpec((SEQ, 2 * LANES), lambda i, j: (i, j)),
                           pl.BlockSpec((SEQ, 2 * LANES), lambda i, j: (i, j)), True, "latent_attention")

    o_c = _dense_attention(qc, kc, vc,
                           pl.BlockSpec((SEQ, LANES), lambda i, j: (i, j)),
                           pl.BlockSpec((SEQ, LANES), lambda i, j: (i, j // 2)),
                           pl.BlockSpec((SEQ, LANES), lambda i, j: (i, j // 2)), False, "axial_attention")

    o_d = _sink_attention(proj4, _alibi_slopes(D_Q_HEADS), p["sink_d"][l].astype(jnp.float32))

    return _merge(xn, (o_a, o_b, o_c, o_d), p["w_all"], p["b_gate"], p["w_branch"], l)


def _layer(h, xn, l, p, tabs, g_next, final):
    t = h.shape[0]
    b = t // SEQ
    mixed = _token_mixer(xn, l, p, tabs)
    h, x_packed, probs = _out_proj(mixed, p["w_out"], h, p["g_ffn_norm"], p["wr_hi"], p["wr_lo"], l)
    probs_t = jnp.transpose(probs.reshape(b, SEQ, LANES)[:, :, :N_EXPERTS], (0, 2, 1)).reshape(b * N_EXPERTS, SEQ)
    idx, gate = _routing(probs_t)
    xe = _gather(idx.reshape(b, N_EXPERTS, 1, CAPACITY), x_packed.reshape(b, SEQ, D_MODEL // 2))
    ye = _ffn(xe, gate.reshape(b, N_EXPERTS, CAPACITY, 1), p["w_exp_gate"], p["w_exp_up"], p["w_exp_down"], l)
    outs = _combine(idx.reshape(b, 1, N_EXPERTS * CAPACITY), ye.reshape(b, N_EXPERTS * CAPACITY, D_MODEL),
                    h.reshape(b, SEQ, D_MODEL), g_next, final)
    return [o.reshape(t, D_MODEL) for o in outs]


def _prepare_params(w_in, w_mla_uq, w_mla_ukv, w_branch, w_out, w_router, w_exp_gate, w_exp_up, w_exp_down):
    bf = jnp.bfloat16
    p = {}
    p["w_all"] = _relayout_w_in(w_in)
    p["w_mla_uq"] = jnp.stack([_mla_uq_layout(w_mla_uq[l]) for l in range(DEPTH)]).astype(bf)
    p["w_mla_ukv"] = jnp.stack([_mla_ukv_layout(w_mla_ukv[l]) for l in range(DEPTH)]).astype(bf)
    p["w_branch"] = w_branch.astype(bf)
    p["w_out"] = w_out.astype(bf)
    wr = jnp.pad(w_router, ((0, 0), (0, 0), (0, LANES - N_EXPERTS)))
    p["wr_hi"] = wr.astype(bf)
    p["wr_lo"] = (wr - p["wr_hi"].astype(jnp.float32)).astype(bf)
    p["w_exp_gate"] = w_exp_gate
    p["w_exp_up"] = w_exp_up
    p["w_exp_down"] = w_exp_down
    return p


def kernel(x, w_in, b_gate, g_attn_norm, g_ffn_norm, g_mla_q, g_mla_kv, w_mla_uq, w_mla_ukv, g_c_q, g_c_k, sink_d,
           w_branch, w_out, w_router, w_exp_gate, w_exp_up, w_exp_down, g_final):
    b, s, d = x.shape
    assert (s, d) == (SEQ, D_MODEL)
    p = _prepare_params(w_in, w_mla_uq, w_mla_ukv, w_branch, w_out, w_router, w_exp_gate, w_exp_up, w_exp_down)
    p.update(b_gate=b_gate, g_attn_norm=g_attn_norm, g_ffn_norm=g_ffn_norm.reshape(DEPTH, 1, D_MODEL),
             g_mla_q=g_mla_q, g_mla_kv=g_mla_kv, g_c_q=g_c_q, g_c_k=g_c_k, sink_d=sink_d)
    pos = np.arange(SEQ)
    rows, cols = pos // GRID_W, pos % GRID_W
    tabs = {
        "mla": _rope_tables([pos], MLA_ROPE, [MLA_NOPE]),
        "axial": _rope_tables([rows, cols, rows, cols], HEAD_DIM // 2, [0, 32, 64, 96]),
    }
    h = x.reshape(b * s, d)
    xn = _rmsnorm(h, g_attn_norm[0], jnp.bfloat16)
    for l in range(DEPTH - 1):
        h, xn = _layer(h, xn, l, p, tabs, g_attn_norm[l + 1], False)
    (out,) = _layer(h, xn, DEPTH - 1, p, tabs, g_final, True)
    return out.reshape(b, s, d)
```
